```python
import math, functools
import jax, jax.numpy as jnp
from jax import lax
import numpy as np

D_MODEL = 1024
BATCH = 8
SEQ = 4096
DEPTH = 2
DEC_BATCH = 32
DEC_SEQ = 4
PAST_LEN = 16384
PAGE_SIZE = 128

D_ATT = D_MODEL // 2
D_SSM = D_MODEL - D_ATT
HEAD_DIM = 64
N_HEADS = D_ATT // HEAD_DIM
SSM_GROUP = 16
N_SSM_GROUPS = D_SSM // SSM_GROUP
SSM_STATE = 64
BRANCHES = ((128, 1), (512, 4), (2048, 16))
W_MAX = max(w for w, _ in BRANCHES)
N_BUCKETS = 32
MAX_DISTANCE = W_MAX
D_FF = ((8 * D_MODEL // 3 + 127) // 128) * 128
D_PLE = 256
D_IN = 3 * D_ATT + D_SSM
BLOCK_Q = 128
EPS = 1e-6
NEG = -1e30

kernel_name = 'hybrid_dilated_attn_s5_decoder_step'


def rmsnorm(x, g):
    xf = x.astype(jnp.float32)
    y = xf * lax.rsqrt(jnp.mean(xf * xf, axis=-1, keepdims=True) + EPS)
    return (y * g.astype(jnp.float32)).astype(x.dtype)


def swiglu(x, wg, wu, wd):
    return (jax.nn.silu(x @ wg) * (x @ wu)) @ wd


def t5_bucket(dist):
    dist = np.asarray(dist, dtype=np.int64)
    exact = N_BUCKETS // 2
    ratio = np.log(np.maximum(dist, 1) / exact) / np.log(MAX_DISTANCE / exact)
    large = np.minimum(exact + (ratio * (N_BUCKETS - exact)).astype(np.int64), N_BUCKETS - 1)
    return np.where(dist < exact, dist, large).astype(np.int32)


def branch_bias(rel_bias, window, dilation):
    n = window // dilation
    return rel_bias[t5_bucket(np.arange(n + 1) * dilation)].T.astype(jnp.float32)


def softmax_stats(logits):
    m = jnp.max(logits, axis=-1, keepdims=True)
    e = jnp.exp(logits - m)
    s = jnp.sum(e, axis=-1, keepdims=True)
    return e / s, (m + jnp.log(s))[..., 0]


def _strided(t, d):
    b, s = t.shape[:2]
    return t.reshape(b, s // d, d, *t.shape[2:]).swapaxes(1, 2).reshape(b * d, s // d, *t.shape[2:])


def _unstrided(t, b, d):
    ls = t.shape[1]
    return t.reshape(b, d, ls, *t.shape[2:]).swapaxes(1, 2).reshape(b, ls * d, *t.shape[2:])


def dilated_branch_prompt(q, k, v, bias, d, n):
    bsz = q.shape[0]
    qs, ks, vs = _strided(q, d), _strided(k, d), _strided(v, d)
    nseq, ls = qs.shape[:2]
    bq = min(BLOCK_Q, ls)
    nb = -(-ls // bq)
    lp = nb * bq
    qs = jnp.pad(qs, ((0, 0), (0, lp - ls), (0, 0), (0, 0)))
    ks = jnp.pad(ks, ((0, 0), (n, lp - ls), (0, 0), (0, 0)))
    vs = jnp.pad(vs, ((0, 0), (n, lp - ls), (0, 0), (0, 0)))
    idx = np.arange(nb)[:, None] * bq + np.arange(bq + n)[None, :]
    kb, vb = ks[:, idx], vs[:, idx]
    qb = qs.reshape(nseq, nb, bq, N_HEADS, HEAD_DIM)
    logits = jnp.einsum('nbqhd,nbkhd->nbhqk', qb, kb,
                        preferred_element_type=jnp.float32) / math.sqrt(HEAD_DIM)
    step = np.arange(bq)[:, None] - np.arange(bq + n)[None, :] + n
    valid = ((step >= 0) & (step <= n))[None] & (idx - n >= 0)[:, None, :]
    logits = logits + bias[:, np.clip(step, 0, n)][None, None]
    logits = jnp.where(valid[None, :, None], logits, NEG)
    probs, lse = softmax_stats(logits)
    o = jnp.einsum('nbhqk,nbkhd->nbqhd', probs, vb.astype(jnp.float32))
    o = o.reshape(nseq, lp, N_HEADS, HEAD_DIM)[:, :ls]
    lse = lse.transpose(0, 1, 3, 2).reshape(nseq, lp, N_HEADS)[:, :ls]
    return _unstrided(o, bsz, d), _unstrided(lse, bsz, d)


def dilated_branch_sample(q, k_ext, v_ext, bias, d, n):
    t_new = q.shape[1]
    past = k_ext.shape[1] - t_new
    idx = past + np.arange(t_new)[:, None] - np.arange(n + 1)[None, :] * d
    valid = idx >= 0
    idx = np.maximum(idx, 0)
    kg, vg = k_ext[:, idx], v_ext[:, idx]
    logits = jnp.einsum('bthd,btkhd->bhtk', q, kg,
                        preferred_element_type=jnp.float32) / math.sqrt(HEAD_DIM)
    logits = logits + bias[None, :, None, :]
    logits = jnp.where(valid[None, None], logits, NEG)
    probs, lse = softmax_stats(logits)
    o = jnp.einsum('bhtk,btkhd->bthd', probs, vg.astype(jnp.float32))
    return o, lse.transpose(0, 2, 1)


def merge_branches(outs, lses):
    w = jax.nn.softmax(jnp.stack(lses, axis=0), axis=0)
    return jnp.sum(w[..., None] * jnp.stack(outs, axis=0), axis=0)


def attend_prompt(q, k, v, rel_bias):
    outs, lses = [], []
    for window, dil in BRANCHES:
        o, l = dilated_branch_prompt(q, k, v, branch_bias(rel_bias, window, dil), dil, window // dil)
        outs.append(o)
        lses.append(l)
    keep = min(W_MAX, q.shape[1])
    return merge_branches(outs, lses), k[:, -keep:], v[:, -keep:]


def attend_sample(q, k, v, rel_bias, cache_k, cache_v):
    buf = cache_k.shape[1]
    k_ext = jnp.concatenate([cache_k.astype(k.dtype), k], axis=1)
    v_ext = jnp.concatenate([cache_v.astype(v.dtype), v], axis=1)
    outs, lses = [], []
    for window, dil in BRANCHES:
        o, l = dilated_branch_sample(q, k_ext, v_ext, branch_bias(rel_bias, window, dil), dil, window // dil)
        outs.append(o)
        lses.append(l)
    return merge_branches(outs, lses), k_ext[:, -buf:], v_ext[:, -buf:]


def s5_mixer(u, h0, a_re, a_im, log_dt, b_re, b_im, c_re, c_im, d_skip, w_glu, b_glu):
    f32 = jnp.float32
    bsz, seq, _ = u.shape
    uf = u.astype(f32)
    lam = lax.complex(a_re.astype(f32), a_im.astype(f32))
    dt = jnp.exp(log_dt.astype(f32))[:, None]
    abar = jnp.exp(lam * dt)
    bbar = ((abar - 1.0) / lam)[:, :, None] * lax.complex(b_re.astype(f32), b_im.astype(f32))
    ug = uf.reshape(bsz, seq, N_SSM_GROUPS, SSM_GROUP).astype(jnp.complex64)
    bu = jnp.einsum('blgi,gni->blgn', ug, bbar)
    bu = bu.at[:, 0].add(abar * h0)
    a = jnp.broadcast_to(abar, (1, seq) + abar.shape)

    def combine(left, right):
        a_l, b_l = left
        a_r, b_r = right
        return a_r * a_l, a_r * b_l + b_r

    _, h = lax.associative_scan(combine, (a, bu), axis=1)
    c = lax.complex(c_re.astype(f32), c_im.astype(f32))
    y = jnp.real(jnp.einsum('blgn,gon->blgo', h, c)).reshape(bsz, seq, D_SSM) + d_skip.astype(f32) * uf
    z = jax.nn.gelu(y)
    out = z * jax.nn.sigmoid(z @ w_glu.astype(f32) + b_glu.astype(f32))
    return out.astype(u.dtype), h[:, -1]


def trunk_layer(x, p, lw, rel_bias, attend, h0):
    bsz, seq, _ = x.shape
    h = rmsnorm(x, lw['norm_ffn'][0])
    x = x + 0.5 * swiglu(h, lw['ffn_w_gate'][0], lw['ffn_w_up'][0], lw['ffn_w_down'][0])
    h = rmsnorm(x, lw['norm_mix'])
    q, k, v, u = jnp.split(h @ lw['w_in'], [D_ATT, 2 * D_ATT, 3 * D_ATT], axis=-1)
    heads = (bsz, seq, N_HEADS, HEAD_DIM)
    att, k_state, v_state = attend(q.reshape(heads), k.reshape(heads), v.reshape(heads), rel_bias)
    ssm, h_last = s5_mixer(u, h0, lw['ssm_a_re'], lw['ssm_a_im'], lw['ssm_log_dt'], lw['ssm_b_re'],
                           lw['ssm_b_im'], lw['ssm_c_re'], lw['ssm_c_im'], lw['ssm_d'], lw['w_glu'], lw['b_glu'])
    mixed = jnp.concatenate([rmsnorm(att.reshape(bsz, seq, D_ATT).astype(x.dtype), lw['norm_att_out']),
                             rmsnorm(ssm, lw['norm_ssm_out'])], axis=-1)
    x = x + mixed @ lw['w_out']
    h = rmsnorm(x, lw['norm_ffn'][1])
    x = x + 0.5 * swiglu(h, lw['ffn_w_gate'][1], lw['ffn_w_up'][1], lw['ffn_w_down'][1])
    gate = jax.nn.sigmoid(rmsnorm(x, lw['norm_ple']) @ lw['w_ple_gate'])
    x = x + gate * (p.astype(x.dtype) @ lw['w_ple_proj'])
    return x, k_state, v_state, h_last


def setup_inputs(seed: int = 0) -> dict:
    key = jax.random.key(seed)
    ks = iter(jax.random.split(key, 40))
    f32 = jnp.float32

    def nrm(shape, scale):
        return scale * jax.random.normal(next(ks), shape, f32)

    def gain(shape):
        return 1.0 + 0.05 * jax.random.normal(next(ks), shape, f32)

    cache_len = min(W_MAX, PAST_LEN)
    G, N, I = N_SSM_GROUPS, SSM_STATE, SSM_GROUP
    return {
        'x_prompt': nrm((BATCH, SEQ, D_MODEL), 1.0),
        'x_sample': nrm((DEC_BATCH, DEC_SEQ, D_MODEL), 1.0),
        'p_prompt': nrm((DEPTH, BATCH, SEQ, D_PLE), 1.0),
        'p_sample': nrm((DEPTH, DEC_BATCH, DEC_SEQ, D_PLE), 1.0),
        'cache_k': nrm((DEPTH, DEC_BATCH, cache_len, N_HEADS, HEAD_DIM), 1.0),
        'cache_v': nrm((DEPTH, DEC_BATCH, cache_len, N_HEADS, HEAD_DIM), 1.0),
        'state_ssm_re': nrm((DEPTH, DEC_BATCH, G, N), 0.1),
        'state_ssm_im': nrm((DEPTH, DEC_BATCH, G, N), 0.1),
        'rel_bias': nrm((N_BUCKETS, N_HEADS), 0.5),
        'w_in': nrm((DEPTH, D_MODEL, D_IN), D_MODEL ** -0.5),
        'w_out': nrm((DEPTH, D_MODEL, D_MODEL), D_MODEL ** -0.5),
        'norm_mix': gain((DEPTH, D_MODEL)),
        'norm_att_out': gain((DEPTH, D_ATT)),
        'norm_ssm_out': gain((DEPTH, D_SSM)),
        'norm_ffn': gain((DEPTH, 2, D_MODEL)),
        'ffn_w_gate': nrm((DEPTH, 2, D_MODEL, D_FF), D_MODEL ** -0.5),
        'ffn_w_up': nrm((DEPTH, 2, D_MODEL, D_FF), D_MODEL ** -0.5),
        'ffn_w_down': nrm((DEPTH, 2, D_FF, D_MODEL), D_FF ** -0.5),
        'ssm_a_re': -0.5 * jnp.exp(nrm((DEPTH, G, N), 0.02)),
        'ssm_a_im': jnp.pi * jnp.arange(N, dtype=f32)[None, None, :] + nrm((DEPTH, G, N), 0.02),
        'ssm_log_dt': jax.random.uniform(next(ks), (DEPTH, G), f32, math.log(1e-3), math.log(1e-1)),
        'ssm_b_re': nrm((DEPTH, G, N, I), (2 * I) ** -0.5),
        'ssm_b_im': nrm((DEPTH, G, N, I), (2 * I) ** -0.5),
        'ssm_c_re': nrm((DEPTH, G, I, N), (2 * N) ** -0.5),
        'ssm_c_im': nrm((DEPTH, G, I, N), (2 * N) ** -0.5),
        'ssm_d': nrm((DEPTH, D_SSM), 1.0),
        'w_glu': nrm((DEPTH, D_SSM, D_SSM), D_SSM ** -0.5),
        'b_glu': nrm((DEPTH, D_SSM), 0.01),
        'norm_ple': gain((DEPTH, D_MODEL)),
        'w_ple_gate': nrm((DEPTH, D_MODEL, D_MODEL), D_MODEL ** -0.5),
        'w_ple_proj': nrm((DEPTH, D_PLE, D_MODEL), D_PLE ** -0.5),
        'norm_final': gain((D_MODEL,)),
    }


def reference(x_prompt, x_sample, p_prompt, p_sample, cache_k, cache_v, state_ssm_re, state_ssm_im,
              rel_bias, w_in, w_out, norm_mix, norm_att_out, norm_ssm_out, norm_ffn, ffn_w_gate,
              ffn_w_up, ffn_w_down, ssm_a_re, ssm_a_im, ssm_log_dt, ssm_b_re, ssm_b_im, ssm_c_re,
              ssm_c_im, ssm_d, w_glu, b_glu, norm_ple, w_ple_gate, w_ple_proj, norm_final):
    f32 = jnp.float32
    xp, xs = x_prompt, x_sample
    kp, vp, rp, ip = [], [], [], []
    kss, vss, rss, iss = [], [], [], []
    for i in range(DEPTH):
        lw = dict(w_in=w_in[i], w_out=w_out[i], norm_mix=norm_mix[i], norm_att_out=norm_att_out[i],
                  norm_ssm_out=norm_ssm_out[i], norm_ffn=norm_ffn[i], ffn_w_gate=ffn_w_gate[i],
                  ffn_w_up=ffn_w_up[i], ffn_w_down=ffn_w_down[i], ssm_a_re=ssm_a_re[i],
                  ssm_a_im=ssm_a_im[i], ssm_log_dt=ssm_log_dt[i], ssm_b_re=ssm_b_re[i],
                  ssm_b_im=ssm_b_im[i], ssm_c_re=ssm_c_re[i], ssm_c_im=ssm_c_im[i], ssm_d=ssm_d[i],
                  w_glu=w_glu[i], b_glu=b_glu[i], norm_ple=norm_ple[i], w_ple_gate=w_ple_gate[i],
                  w_ple_proj=w_ple_proj[i])
        h0p = jnp.zeros((xp.shape[0], N_SSM_GROUPS, SSM_STATE), jnp.complex64)
        xp, k_new, v_new, h_last = trunk_layer(xp, p_prompt[i], lw, rel_bias, attend_prompt, h0p)
        kp.append(k_new)
        vp.append(v_new)
        rp.append(jnp.real(h_last))
        ip.append(jnp.imag(h_last))
        h0s = lax.complex(state_ssm_re[i].astype(f32), state_ssm_im[i].astype(f32))
        attend_s = functools.partial(attend_sample, cache_k=cache_k[i], cache_v=cache_v[i])
        xs, k_new, v_new, h_last = trunk_layer(xs, p_sample[i], lw, rel_bias, attend_s, h0s)
        kss.append(k_new)
        vss.append(v_new)
        rss.append(jnp.real(h_last))
        iss.append(jnp.imag(h_last))
    y_prompt = rmsnorm(xp, norm_final)
    y_sample = rmsnorm(xs, norm_final)
    return (y_prompt, y_sample, jnp.stack(kp), jnp.stack(vp), jnp.stack(rp), jnp.stack(ip),
            jnp.stack(kss), jnp.stack(vss), jnp.stack(rss), jnp.stack(iss))
```

```python
import functools
import math

import numpy as np
import jax
import jax.numpy as jnp
from jax import lax
from jax.experimental import pallas as pl
from jax.experimental.pallas import tpu as pltpu

F32 = jnp.float32
BF16 = jnp.bfloat16

D_MODEL = 1024
D_ATT = 512
D_SSM = 512
HEAD_DIM = 64
N_HEADS = 8
SSM_GROUP = 16
N_GROUPS = 32
SSM_STATE = 64
BRANCHES = ((128, 1), (512, 4), (2048, 16))
N_STEP = 128
N_BUCKETS = 32
MAX_DISTANCE = 2048
D_FF = 2816
D_PLE = 256
D_IN = 3 * D_ATT + D_SSM
EPS = 1e-6
NEG = -1e30

LANE = 128
MXU = 256
GROUPS_PER_TILE = LANE // SSM_GROUP
N_JT = D_SSM // LANE
PLANE = GROUPS_PER_TILE * SSM_STATE
TOK_PER_PAIR = MXU // LANE
FF_CHUNK = 256
VMEM_LIMIT = 56 * 1024 * 1024

assert all(w // d == N_STEP for w, d in BRANCHES)
assert D_FF % FF_CHUNK == 0


def _rms(x, g):
    return x * lax.rsqrt(jnp.mean(x * x, axis=-1, keepdims=True) + EPS) * g


def _dot(a, b):
    return jnp.dot(a, b, preferred_element_type=F32)


def _ffn(x, g, wg_ref, wu_ref, wd_ref):
    h = _rms(x, g).astype(BF16)
    acc = jnp.zeros(x.shape, F32)
    for c in range(D_FF // FF_CHUNK):
        sl = slice(c * FF_CHUNK, (c + 1) * FF_CHUNK)
        gate = _dot(h, wg_ref[:, sl])
        up = _dot(h, wu_ref[:, sl])
        act = (jax.nn.silu(gate) * up).astype(BF16)
        acc = acc + _dot(act, wd_ref[sl, :])
    return x + 0.5 * acc


def _const_spec(shape):
    nd = len(shape)
    return pl.BlockSpec(shape, lambda *_: (0,) * nd, pipeline_mode=pl.Buffered(1))


def _params(n_axes):
    return pltpu.CompilerParams(dimension_semantics=("arbitrary",) * n_axes,
                                vmem_limit_bytes=VMEM_LIMIT)


def _premix_kernel(x_ref, nf_ref, wg_ref, wu_ref, wd_ref, nm_ref, win_ref,
                   x1_ref, q_ref, k_ref, v_ref, kf_ref, vf_ref, u_ref):
    x1 = _ffn(x_ref[...], nf_ref[...], wg_ref, wu_ref, wd_ref)
    x1_ref[...] = x1
    h = _rms(x1, nm_ref[...]).astype(BF16)
    qkvu = _dot(h, win_ref[...])
    q_ref[...] = (qkvu[:, :D_ATT] * (1.0 / math.sqrt(HEAD_DIM))).astype(BF16)
    kf = qkvu[:, D_ATT:2 * D_ATT]
    vf = qkvu[:, 2 * D_ATT:3 * D_ATT]
    kf_ref[...] = kf
    vf_ref[...] = vf
    k_ref[...] = kf.astype(BF16)
    v_ref[...] = vf.astype(BF16)
    for j in range(N_JT):
        u_ref[j] = qkvu[:, 3 * D_ATT + j * LANE:3 * D_ATT + (j + 1) * LANE]


def _premix(x, nf, wg, wu, wd, nm, win, tm):
    m = x.shape[0]
    row = lambda w: pl.BlockSpec((tm, w), lambda i: (i, 0))
    return pl.pallas_call(
        _premix_kernel,
        grid=(m // tm,),
        in_specs=[row(D_MODEL), _const_spec(nf.shape), _const_spec(wg.shape), _const_spec(wu.shape),
                  _const_spec(wd.shape), _const_spec(nm.shape), _const_spec(win.shape)],
        out_specs=[row(D_MODEL), row(D_ATT), row(D_ATT), row(D_ATT), row(D_ATT), row(D_ATT),
                   pl.BlockSpec((N_JT, tm, LANE), lambda i: (0, i, 0))],
        out_shape=[jax.ShapeDtypeStruct((m, D_MODEL), F32),
                   jax.ShapeDtypeStruct((m, D_ATT), BF16),
                   jax.ShapeDtypeStruct((m, D_ATT), BF16),
                   jax.ShapeDtypeStruct((m, D_ATT), BF16),
                   jax.ShapeDtypeStruct((m, D_ATT), F32),
                   jax.ShapeDtypeStruct((m, D_ATT), F32),
                   jax.ShapeDtypeStruct((N_JT, m, LANE), F32)],
        compiler_params=_params(1),
        name="premix",
    )(x, nf, wg, wu, wd, nm, win)


def _attn_prompt_kernel(q_ref, kc_ref, kp_ref, vc_ref, vp_ref, tab_ref, o_ref, l_ref, kbuf, vbuf, *, qb):
    first = pl.program_id(2) == 0
    kbuf[0:N_STEP, :] = kp_ref[0]
    kbuf[N_STEP:, :] = kc_ref[0]
    vbuf[0:N_STEP, :] = vp_ref[0]
    vbuf[N_STEP:, :] = vc_ref[0]
    nk = 2 * N_STEP
    lo_half = lax.broadcasted_iota(jnp.int32, (N_STEP, LANE), 1) < HEAD_DIM
    keep_lo = jnp.where(lo_half, 1.0, 0.0).astype(BF16)
    keep_hi = jnp.where(lo_half, 0.0, 1.0).astype(BF16)
    col = lax.broadcasted_iota(jnp.int32, (N_STEP, nk), 1)
    before_start = jnp.where((col < N_STEP) & first, NEG, 0.0)

    def block(qi, carry):
        r0 = pl.multiple_of(qi * N_STEP, N_STEP)
        edge = jnp.where(qi == 0, before_start, 0.0)
        for hp in range(N_HEADS // 2):
            lanes = slice(hp * LANE, (hp + 1) * LANE)
            qp = q_ref[0, pl.ds(r0, N_STEP), lanes]
            kp = kbuf[pl.ds(r0, nk), lanes]
            vp = vbuf[pl.ds(r0, nk), lanes]
            o_pair, l_pair = None, None
            for hh in range(2):
                qm = qp * (keep_lo if hh == 0 else keep_hi)
                s = lax.dot_general(qm, kp, (((1,), (1,)), ((), ())), preferred_element_type=F32)
                s = s + tab_ref[2 * hp + hh] + edge
                m = jnp.max(s, axis=-1, keepdims=True)
                e = jnp.exp(s - m)
                den = jnp.sum(e, axis=-1, keepdims=True)
                o = _dot(e.astype(BF16), vp) / den
                lse = jnp.broadcast_to(m + jnp.log(den), (N_STEP, LANE))
                if hh == 0:
                    o_pair, l_pair = o, lse
                else:
                    o_pair = jnp.where(lo_half, o_pair, o)
                    l_pair = jnp.where(lo_half, l_pair, lse)
            o_ref[0, pl.ds(r0, N_STEP), lanes] = o_pair
            l_ref[0, pl.ds(r0, N_STEP), lanes] = l_pair
        return carry

    lax.fori_loop(0, qb // N_STEP, block, 0)


def _attn_prompt_branch(q, k, v, tab, batch, seq, dil):
    sub = seq // dil
    qb = min(sub, 512)
    assert sub % qb == 0 and qb % N_STEP == 0
    per = qb // N_STEP
    view = lambda a: a.reshape(batch, sub, dil * D_ATT)
    cur = pl.BlockSpec((1, qb, D_ATT), lambda b, r, i: (b, i, r))
    prev = pl.BlockSpec((1, N_STEP, D_ATT), lambda b, r, i: (b, jnp.maximum(i * per - 1, 0), r))
    o, l = pl.pallas_call(
        functools.partial(_attn_prompt_kernel, qb=qb),
        grid=(batch, dil, sub // qb),
        in_specs=[cur, cur, prev, cur, prev, _const_spec(tab.shape)],
        out_specs=[cur, cur],
        out_shape=[jax.ShapeDtypeStruct((batch, sub, dil * D_ATT), F32)] * 2,
        scratch_shapes=[pltpu.VMEM((qb + N_STEP, D_ATT), BF16)] * 2,
        compiler_params=_params(3),
        name=f"attn_prompt_d{dil}",
    )(view(q), view(k), view(k), view(v), view(v), tab)
    return o.reshape(batch * seq, D_ATT), l.reshape(batch * seq, D_ATT)


Q_PAD = 8
NEW_SLOT = 16


def _sample_layout(cache_len, t_new):
    assert t_new <= Q_PAD and t_new <= NEW_SLOT
    (w1, d1), (w4, d4), (w16, d16) = BRANCHES
    assert d1 == 1 and cache_len % d16 == 0 and t_new <= d16 and cache_len >= w16
    win1 = w1
    win4 = w4
    n16 = cache_len // d16
    off4 = win1
    off16 = off4 + win4
    offn = off16 + t_new * n16
    total = -(-(offn + 3 * NEW_SLOT) // LANE) * LANE
    branch_of = np.full((t_new, total), -1, np.int32)
    step_of = np.zeros((t_new, total), np.int32)
    for i in range(t_new):
        col_of = [{}, {}, {}]
        for c in range(win1):
            col_of[0][cache_len - win1 + c] = c
        for c in range(win4):
            col_of[1][cache_len - win4 + c] = off4 + c
        for g in range(n16):
            col_of[2][g * d16 + i] = off16 + i * n16 + g
        for bi in range(3):
            for t in range(t_new):
                col_of[bi][cache_len + t] = offn + bi * NEW_SLOT + t
        for bi, (w, d) in enumerate(BRANCHES):
            for j in range(w // d + 1):
                ext = cache_len + i - j * d
                assert ext >= 0
                c = col_of[bi][ext]
                assert branch_of[i, c] == -1
                branch_of[i, c] = bi
                step_of[i, c] = j
    return dict(win1=win1, win4=win4, n16=n16, off4=off4, off16=off16, offn=offn, total=total,
                branch_of=branch_of, step_of=step_of)


def _attn_sample_kernel(q_ref, kn_ref, vn_ref, k1_ref, k4_ref, k16_ref, v1_ref, v4_ref, v16_ref, tab_ref,
                        o_ref, qs, kbuf, vbuf, *, t_new, lay):
    off4, off16, offn, n16, total = lay["off4"], lay["off16"], lay["offn"], lay["n16"], lay["total"]
    q = q_ref[0].astype(F32)
    qs[...] = jnp.zeros(qs.shape, F32)
    for buf, new_ref, w1_ref, w4_ref, w16_ref in ((kbuf, kn_ref, k1_ref, k4_ref, k16_ref),
                                                   (vbuf, vn_ref, v1_ref, v4_ref, v16_ref)):
        new = new_ref[0].astype(F32)
        for h in range(N_HEADS):
            buf[h, 0:off4, :] = w1_ref[0, :, h, :]
            buf[h, off4:off16, :] = w4_ref[0, :, h, :]
            for r in range(t_new):
                buf[h, off16 + r * n16:off16 + (r + 1) * n16, :] = w16_ref[0, :, r, h, :]
            buf[h, offn:, :] = jnp.zeros((total - offn, HEAD_DIM), F32)
            for bi in range(3):
                buf[h, offn + bi * NEW_SLOT:offn + bi * NEW_SLOT + t_new, :] = new[:, h * HEAD_DIM:(h + 1) * HEAD_DIM]
    for h in range(N_HEADS):
        qs[h, 0:t_new, :] = q[:, h * HEAD_DIM:(h + 1) * HEAD_DIM]
        s = lax.dot_general(qs[h].astype(BF16), kbuf[h].astype(BF16), (((1,), (1,)), ((), ())),
                            preferred_element_type=F32)
        s = s + tab_ref[h]
        m = jnp.max(s, axis=-1, keepdims=True)
        e = jnp.exp(s - m)
        den = jnp.sum(e, axis=-1, keepdims=True)
        o = _dot(e.astype(BF16), vbuf[h].astype(BF16)) / den
        o_ref[0, :, h * HEAD_DIM:(h + 1) * HEAD_DIM] = o[:t_new]


def _attn_sample(q, k_new, v_new, cache_k, cache_v, tab, lay, batch, t_new):
    cache_len = cache_k.shape[1]
    d16 = BRANCHES[2][1]
    win1, win4, n16, total = lay["win1"], lay["win4"], lay["n16"], lay["total"]
    tok = pl.BlockSpec((1, t_new, D_ATT), lambda b: (b, 0, 0))
    s1 = pl.BlockSpec((1, win1, N_HEADS, HEAD_DIM), lambda b: (b, cache_len // win1 - 1, 0, 0))
    s4 = pl.BlockSpec((1, win4, N_HEADS, HEAD_DIM), lambda b: (b, cache_len // win4 - 1, 0, 0))
    s16 = pl.BlockSpec((1, n16, t_new, N_HEADS, HEAD_DIM), lambda b: (b, 0, 0, 0, 0))
    tokv = lambda a: a.reshape(batch, t_new, D_ATT)
    strided = lambda a: a.reshape(batch, n16, d16, N_HEADS, HEAD_DIM)
    out = pl.pallas_call(
        functools.partial(_attn_sample_kernel, t_new=t_new, lay=lay),
        grid=(batch,),
        in_specs=[tok, tok, tok, s1, s4, s16, s1, s4, s16, _const_spec(tab.shape)],
        out_specs=tok,
        out_shape=jax.ShapeDtypeStruct((batch, t_new, D_ATT), F32),
        scratch_shapes=[pltpu.VMEM((N_HEADS, Q_PAD, HEAD_DIM), F32),
                        pltpu.VMEM((N_HEADS, total, HEAD_DIM), F32),
                        pltpu.VMEM((N_HEADS, total, HEAD_DIM), F32)],
        compiler_params=_params(1),
        name="attn_sample",
    )(tokv(q), tokv(k_new), tokv(v_new), cache_k, cache_k, strided(cache_k),
      cache_v, cache_v, strided(cache_v), tab)
    return out.reshape(batch * t_new, D_ATT)


def _ssm_kernel(u_ref, h0_ref, w_ref, pw_ref, qw_ref, a_ref, d_ref, y_ref, hl_ref, z_buf, hb_buf,
                *, chunk, scan_rows):
    n_pair = chunk // TOK_PER_PAIR
    u = u_ref[0]
    ub = u.astype(BF16)
    z = _dot(ub, pw_ref[0])
    ar = a_ref[0, 0:1, :]
    ai = a_ref[0, 1:2, :]
    if scan_rows:
        z_buf[...] = z

        def step(r, h):
            hr, hi = h
            hb_buf[pl.ds(r, 1), 0:PLANE] = hr
            hb_buf[pl.ds(r, 1), PLANE:] = hi
            zr = z_buf[pl.ds(r, 1), 0:PLANE]
            zi = z_buf[pl.ds(r, 1), PLANE:]
            return ar * hr - ai * hi + zr, ar * hi + ai * hr + zi

        h0 = h0_ref[0, 0]
        hr, hi = lax.fori_loop(0, u.shape[0], step, (h0[:, 0:PLANE], h0[:, PLANE:]))
        hl_ref[0, 0, :, 0:PLANE] = hr
        hl_ref[0, 0, :, PLANE:] = hi
        hb = hb_buf[...].astype(BF16)
    else:
        h0 = h0_ref[0, 0]
        hr, hi = h0[:, 0:PLANE], h0[:, PLANE:]
        hl_ref[0, 0, :, 0:PLANE] = ar * hr - ai * hi + z[:, 0:PLANE]
        hl_ref[0, 0, :, PLANE:] = ar * hi + ai * hr + z[:, PLANE:]
        hb = h0.astype(BF16)
    dskip = d_ref[0]
    for b in range(n_pair):
        cols = slice(b * MXU, (b + 1) * MXU)
        acc = _dot(hb, qw_ref[0, b]) + dskip * u[:, cols]
        for a in range(b + 1):
            acc = acc + _dot(ub[:, a * MXU:(a + 1) * MXU], w_ref[0, b - a])
        y_ref[0, :, cols] = acc


def _ssm(u4, h0, tabs, chunk, rows_per_block, scan_rows):
    w, pw, qw, a_pow, dsk = tabs
    n_tok = u4.shape[1]
    rows = n_tok // chunk
    n_blocks = rows // rows_per_block
    seqs = h0.shape[2]
    width = chunk * LANE
    y, h_last = pl.pallas_call(
        functools.partial(_ssm_kernel, chunk=chunk, scan_rows=scan_rows),
        grid=(N_JT, n_blocks),
        in_specs=[pl.BlockSpec((1, rows_per_block, width), lambda j, b: (j, b, 0)),
                  pl.BlockSpec((1, 1, seqs, 2 * PLANE), lambda j, b: (j, b, 0, 0)),
                  pl.BlockSpec((1,) + w.shape[1:], lambda j, b: (j, 0, 0, 0)),
                  pl.BlockSpec((1,) + pw.shape[1:], lambda j, b: (j, 0, 0)),
                  pl.BlockSpec((1,) + qw.shape[1:], lambda j, b: (j, 0, 0, 0)),
                  pl.BlockSpec((1,) + a_pow.shape[1:], lambda j, b: (j, 0, 0)),
                  pl.BlockSpec((1,) + dsk.shape[1:], lambda j, b: (j, 0, 0))],
        out_specs=[pl.BlockSpec((1, rows_per_block, width), lambda j, b: (j, b, 0)),
                   pl.BlockSpec((1, 1, seqs, 2 * PLANE), lambda j, b: (j, b, 0, 0))],
        out_shape=[jax.ShapeDtypeStruct((N_JT, rows, width), F32),
                   jax.ShapeDtypeStruct(h0.shape, F32)],
        scratch_shapes=[pltpu.VMEM((rows_per_block, 2 * PLANE), F32)] * 2,
        compiler_params=_params(2),
        name=f"ssm_chunk{chunk}",
    )(u4.reshape(N_JT, rows, width), h0, w, pw, qw, a_pow, dsk)
    return y.reshape(N_JT, n_tok, LANE), h_last


def _ssm_tables(a_re, a_im, log_dt, b_re, b_im, c_re, c_im, d_skip, chunk):
    hi = lax.Precision.HIGHEST
    dt = jnp.exp(log_dt)[:, None]
    den = a_re * a_re + a_im * a_im
    er, ang = jnp.exp(a_re * dt), a_im * dt
    ab_re, ab_im = er * jnp.cos(ang), er * jnp.sin(ang)
    nr, ni = ab_re - 1.0, ab_im
    f_re, f_im = (nr * a_re + ni * a_im) / den, (ni * a_re - nr * a_im) / den
    bb_re = f_re[..., None] * b_re - f_im[..., None] * b_im
    bb_im = f_re[..., None] * b_im + f_im[..., None] * b_re
    def power(taus):
        taus = jnp.asarray(taus, F32)[:, None, None]
        mag, ang_t = jnp.exp(a_re * dt * taus), a_im * dt * taus
        return mag * jnp.cos(ang_t), mag * jnp.sin(ang_t)

    p_re, p_im = power(np.arange(chunk + 1))
    pr, pi = (x[..., None] for x in power(chunk - 1 - np.arange(chunk)))
    s_re, s_im = pr * bb_re - pi * bb_im, pr * bb_im + pi * bb_re
    qr, qi = p_re[1:, :, None, :], p_im[1:, :, None, :]
    e_re, e_im = c_re * qr - c_im * qi, c_re * qi + c_im * qr
    lr, li = p_re[:chunk, :, None, :], p_im[:chunk, :, None, :]
    cl_re, cl_im = c_re * lr - c_im * li, c_re * li + c_im * lr
    kern = (jnp.einsum("tgon,gni->tgoi", cl_re, bb_re, precision=hi)
            - jnp.einsum("tgon,gni->tgoi", cl_im, bb_im, precision=hi))

    eye = jnp.eye(GROUPS_PER_TILE, dtype=F32)
    n_pair = chunk // TOK_PER_PAIR
    tile = lambda x: x.reshape((x.shape[0], N_JT, GROUPS_PER_TILE) + x.shape[2:])
    kern_t = tile(kern)
    kern_t = jnp.concatenate([kern_t, jnp.zeros_like(kern_t[:1])], axis=0)
    lag = (TOK_PER_PAIR * np.arange(n_pair)[:, None, None]
           + np.arange(TOK_PER_PAIR)[None, None, :] - np.arange(TOK_PER_PAIR)[None, :, None])
    kl = kern_t[lag]
    w = jnp.einsum("dpqjgoi,gh->jdpgiqho", kl, eye)
    w = w.reshape(N_JT, n_pair, MXU, MXU)
    s_t = jnp.stack([tile(s_re), tile(s_im)], axis=0)
    pw = jnp.einsum("ptjgni,gh->jtgiphn", s_t, eye)
    pw = pw.reshape(N_JT, chunk * LANE, 2 * PLANE)
    e_t = jnp.stack([tile(e_re), -tile(e_im)], axis=0)
    qw = jnp.einsum("ptjgon,gh->jpgntho", e_t, eye)
    qw = qw.reshape(N_JT, 2 * PLANE, n_pair, MXU).transpose(0, 2, 1, 3)
    a_pow = jnp.stack([p_re[chunk].reshape(N_JT, PLANE), p_im[chunk].reshape(N_JT, PLANE)], axis=1)
    dsk = jnp.tile(d_skip.reshape(N_JT, 1, LANE), (1, 1, TOK_PER_PAIR))
    return w.astype(BF16), pw.astype(BF16), qw.astype(BF16), a_pow, dsk


def _state_to_tiles(re, im, n_blocks):
    n_seq = re.shape[0]
    t = lambda x: x.reshape(n_blocks, n_seq // n_blocks, N_JT, PLANE).transpose(2, 0, 1, 3)
    return jnp.concatenate([t(re), t(im)], axis=-1)


def _tiles_to_state(h):
    n_seq = h.shape[1] * h.shape[2]
    t = lambda x: x.transpose(1, 2, 0, 3).reshape(n_seq, N_GROUPS, SSM_STATE)
    return t(h[..., :PLANE]), t(h[..., PLANE:])


def _postmix_kernel(*refs, n_att, final):
    it = iter(refs)
    x1_ref = next(it)
    att_refs = [(next(it), next(it)) for _ in range(n_att)] if n_att > 1 else [(next(it), None)]
    y_ref, p_ref = next(it), next(it)
    (natt_ref, nssm_ref, wglu_ref, bglu_ref, wout_ref, nf_ref, wg_ref, wu_ref, wd_ref,
     nple_ref, wpg_ref, wpp_ref) = (next(it) for _ in range(12))
    nfin_ref = next(it) if final else None
    out_ref = next(it)

    if n_att == 1:
        att = att_refs[0][0][...]
    else:
        lses = [l[...] for _, l in att_refs]
        top = functools.reduce(jnp.maximum, lses)
        ws = [jnp.exp(l - top) for l in lses]
        att = sum(w * o[...] for w, (o, _) in zip(ws, att_refs)) / sum(ws)
    a_n = _rms(att, natt_ref[...])
    y = jnp.concatenate([y_ref[j] for j in range(N_JT)], axis=-1)
    z = jax.nn.gelu(y)
    s = z * jax.nn.sigmoid(_dot(z.astype(BF16), wglu_ref[...]) + bglu_ref[...])
    s_n = _rms(s, nssm_ref[...])
    mixed = jnp.concatenate([a_n, s_n], axis=-1).astype(BF16)
    x2 = x1_ref[...] + _dot(mixed, wout_ref[...])
    x3 = _ffn(x2, nf_ref[...], wg_ref, wu_ref, wd_ref)
    gate = jax.nn.sigmoid(_dot(_rms(x3, nple_ref[...]).astype(BF16), wpg_ref[...]))
    x4 = x3 + gate * _dot(p_ref[...].astype(BF16), wpp_ref[...])
    out_ref[...] = _rms(x4, nfin_ref[...]) if final else x4


def _postmix(x1, att_planes, y4, p, weights, nfin, tm):
    m = x1.shape[0]
    n_att = len(att_planes)
    final = nfin is not None
    row = lambda w: pl.BlockSpec((tm, w), lambda i: (i, 0))
    att_args, att_specs = [], []
    for o, l in att_planes:
        att_args.append(o)
        att_specs.append(row(D_ATT))
        if n_att > 1:
            att_args.append(l)
            att_specs.append(row(D_ATT))
    consts = list(weights) + ([nfin] if final else [])
    return pl.pallas_call(
        functools.partial(_postmix_kernel, n_att=n_att, final=final),
        grid=(m // tm,),
        in_specs=[row(D_MODEL)] + att_specs
                 + [pl.BlockSpec((N_JT, tm, LANE), lambda i: (0, i, 0)), row(D_PLE)]
                 + [_const_spec(c.shape) for c in consts],
        out_specs=row(D_MODEL),
        out_shape=jax.ShapeDtypeStruct((m, D_MODEL), F32),
        compiler_params=_params(1),
        name="postmix",
    )(x1, *att_args, y4, p, *consts)


def _t5_bucket(dist):
    dist = np.asarray(dist, dtype=np.int64)
    exact = N_BUCKETS // 2
    ratio = np.log(np.maximum(dist, 1) / exact) / np.log(MAX_DISTANCE / exact)
    large = np.minimum(exact + (ratio * (N_BUCKETS - exact)).astype(np.int64), N_BUCKETS - 1)
    return np.where(dist < exact, dist, large).astype(np.int32)


def _branch_bias(rel_bias, dil):
    return rel_bias[_t5_bucket(np.arange(N_STEP + 1) * dil)].T.astype(F32)


def _prompt_table(bias):
    step = np.arange(N_STEP)[:, None] - np.arange(2 * N_STEP)[None, :] + N_STEP
    valid = (step >= 0) & (step <= N_STEP)
    return jnp.where(valid[None], bias[:, np.clip(step, 0, N_STEP)], NEG)


def _sample_table(biases, lay, t_new):
    total = lay["total"]
    branch_of = np.zeros((Q_PAD, total), np.int32) - 1
    step_of = np.zeros((Q_PAD, total), np.int32)
    branch_of[:t_new], step_of[:t_new] = lay["branch_of"], lay["step_of"]
    stacked = jnp.stack(biases, axis=0)
    vals = stacked[np.maximum(branch_of, 0), :, step_of]
    tab = jnp.where((branch_of >= 0)[..., None], vals, NEG)
    tab = tab.at[t_new:].set(0.0)
    return tab.transpose(2, 0, 1)


def _row_tile(m):
    return min(m, 256)


def kernel(x_prompt, x_sample, p_prompt, p_sample, cache_k, cache_v, state_ssm_re, state_ssm_im,
           rel_bias, w_in, w_out, norm_mix, norm_att_out, norm_ssm_out, norm_ffn, ffn_w_gate,
           ffn_w_up, ffn_w_down, ssm_a_re, ssm_a_im, ssm_log_dt, ssm_b_re, ssm_b_im, ssm_c_re,
           ssm_c_im, ssm_d, w_glu, b_glu, norm_ple, w_ple_gate, w_ple_proj, norm_final):
    depth = w_in.shape[0]
    batch, seq, _ = x_prompt.shape
    dec_batch, dec_seq, _ = x_sample.shape
    cache_len = cache_k.shape[2]
    keep = min(BRANCHES[-1][0], seq)
    chunk_p = 16
    assert seq % (chunk_p * 8) == 0 and dec_seq % TOK_PER_PAIR == 0

    xp = x_prompt.reshape(batch * seq, D_MODEL)
    xs = x_sample.reshape(dec_batch * dec_seq, D_MODEL)
    tm_p, tm_s = _row_tile(xp.shape[0]), _row_tile(xs.shape[0])
    lay = _sample_layout(cache_len, dec_seq)
    biases = [_branch_bias(rel_bias, d) for _, d in BRANCHES]
    tabs_p = [_prompt_table(b) for b in biases]
    tab_s = _sample_table(biases, lay, dec_seq)
    row = lambda v: v.reshape(1, -1)
    bf = lambda a: a.astype(BF16)

    kp, vp, rp, ip, kss, vss, rss, iss = [], [], [], [], [], [], [], []
    for i in range(depth):
        pre = (row(norm_ffn[i, 0]), bf(ffn_w_gate[i, 0]), bf(ffn_w_up[i, 0]), bf(ffn_w_down[i, 0]),
               row(norm_mix[i]), bf(w_in[i]))
        post = (row(norm_att_out[i]), row(norm_ssm_out[i]), bf(w_glu[i]), row(b_glu[i]), bf(w_out[i]),
                row(norm_ffn[i, 1]), bf(ffn_w_gate[i, 1]), bf(ffn_w_up[i, 1]), bf(ffn_w_down[i, 1]),
                row(norm_ple[i]), bf(w_ple_gate[i]), bf(w_ple_proj[i]))
        nfin = row(norm_final) if i == depth - 1 else None
        ssm_par = (ssm_a_re[i], ssm_a_im[i], ssm_log_dt[i], ssm_b_re[i], ssm_b_im[i], ssm_c_re[i],
                   ssm_c_im[i], ssm_d[i])

        x1, q, k, v, kf, vf, u4 = _premix(xp, *pre, tm_p)
        planes = [_attn_prompt_branch(q, k, v, tabs_p[bi], batch, seq, d)
                  for bi, (_, d) in enumerate(BRANCHES)]
        zeros = jnp.zeros((batch, N_GROUPS, SSM_STATE), F32)
        y4, h_last = _ssm(u4, _state_to_tiles(zeros, zeros, batch), _ssm_tables(*ssm_par, chunk_p),
                          chunk_p, seq // chunk_p, True)
        xp = _postmix(x1, planes, y4, p_prompt[i].reshape(batch * seq, D_PLE), post, nfin, tm_p)
        kp.append(kf.reshape(batch, seq, N_HEADS, HEAD_DIM)[:, -keep:])
        vp.append(vf.reshape(batch, seq, N_HEADS, HEAD_DIM)[:, -keep:])
        re, im = _tiles_to_state(h_last)
        rp.append(re)
        ip.append(im)

        x1, q, k, v, kf, vf, u4 = _premix(xs, *pre, tm_s)
        att = _attn_sample(q, k, v, cache_k[i], cache_v[i], tab_s, lay, dec_batch, dec_seq)
        h0 = _state_to_tiles(state_ssm_re[i], state_ssm_im[i], 1)
        y4, h_last = _ssm(u4, h0, _ssm_tables(*ssm_par, dec_seq), dec_seq, dec_batch, False)
        xs = _postmix(x1, [(att, None)], y4, p_sample[i].reshape(dec_batch * dec_seq, D_PLE), post, nfin, tm_s)
        k_new = kf.reshape(dec_batch, dec_seq, N_HEADS, HEAD_DIM)
        v_new = vf.reshape(dec_batch, dec_seq, N_HEADS, HEAD_DIM)
        kss.append(jnp.concatenate([cache_k[i], k_new], axis=1)[:, -cache_len:])
        vss.append(jnp.concatenate([cache_v[i], v_new], axis=1)[:, -cache_len:])
        re, im = _tiles_to_state(h_last)
        rss.append(re)
        iss.append(im)

    return (xp.reshape(batch, seq, D_MODEL), xs.reshape(dec_batch, dec_seq, D_MODEL),
            jnp.stack(kp), jnp.stack(vp), jnp.stack(rp), jnp.stack(ip),
            jnp.stack(kss), jnp.stack(vss), jnp.stack(rss), jnp.stack(iss))
```

```python
import functools
import math

import numpy as np
import jax
import jax.numpy as jnp
from jax import lax
from jax.experimental import pallas as pl
from jax.experimental.pallas import tpu as pltpu

F32 = jnp.float32
BF16 = jnp.bfloat16

D_MODEL = 1024
D_ATT = 512
D_SSM = 512
HEAD_DIM = 64
N_HEADS = 8
SSM_GROUP = 16
N_GROUPS = 32
SSM_STATE = 64
BRANCHES = ((128, 1), (512, 4), (2048, 16))
N_STEP = 128
N_BUCKETS = 32
MAX_DISTANCE = 2048
D_FF = 2816
D_PLE = 256
D_IN = 3 * D_ATT + D_SSM
EPS = 1e-6
NEG = -1e30

LANE = 128
MXU = 256
GROUPS_PER_TILE = LANE // SSM_GROUP
N_JT = D_SSM // LANE
PLANE = GROUPS_PER_TILE * SSM_STATE
TOK_PER_PAIR = MXU // LANE
FF_CHUNK = 256
VMEM_LIMIT = 56 * 1024 * 1024

assert all(w // d == N_STEP for w, d in BRANCHES)
assert D_FF % FF_CHUNK == 0


def _rms(x, g):
    return x * lax.rsqrt(jnp.mean(x * x, axis=-1, keepdims=True) + EPS) * g


def _dot(a, b):
    return jnp.dot(a, b, preferred_element_type=F32)


def _ffn(x, g, wg_ref, wu_ref, wd_ref):
    h = _rms(x, g).astype(BF16)
    acc = jnp.zeros(x.shape, F32)
    for c in range(D_FF // FF_CHUNK):
        sl = slice(c * FF_CHUNK, (c + 1) * FF_CHUNK)
        gate = _dot(h, wg_ref[:, sl])
        up = _dot(h, wu_ref[:, sl])
        act = (jax.nn.silu(gate) * up).astype(BF16)
        acc = acc + _dot(act, wd_ref[sl, :])
    return x + 0.5 * acc


def _perm_matrix(n, d):
    p = np.zeros((n, n), np.float32)
    s, r = np.meshgrid(np.arange(n // d), np.arange(d), indexing="ij")
    p[(r * (n // d) + s).ravel(), (d * s + r).ravel()] = 1.0
    return p


def _permute_f32(pt_ref, x):
    hi = x.astype(BF16)
    rest = x - hi.astype(F32)
    mid = rest.astype(BF16)
    lo = (rest - mid.astype(F32)).astype(BF16)
    pt = pt_ref[...]
    return _dot(pt, hi) + _dot(pt, mid) + _dot(pt, lo)


def _window_copies(src_ref, new_ref, dst_ref, sem):
    depth, shift = src_ref.shape[0], new_ref.shape[2]
    n = src_ref.shape[2] - shift
    copies = [pltpu.make_async_copy(src_ref.at[l, :, pl.ds(shift, n)], dst_ref.at[l, :, pl.ds(0, n)], sem.at[l])
              for l in range(depth)]
    return copies + [pltpu.make_async_copy(new_ref, dst_ref.at[:, :, pl.ds(n, shift)], sem.at[depth])]


def _background(copies):
    @pl.when(pl.program_id(0) == 0)
    def _():
        for c in copies:
            c.start()

    @pl.when(pl.program_id(0) == pl.num_programs(0) - 1)
    def _():
        for c in copies:
            c.wait()


def _const_spec(shape):
    nd = len(shape)
    return pl.BlockSpec(shape, lambda *_: (0,) * nd, pipeline_mode=pl.Buffered(1))


_ANY = pl.BlockSpec(memory_space=pl.ANY)


def _params(n_axes):
    return pltpu.CompilerParams(dimension_semantics=("arbitrary",) * n_axes,
                                vmem_limit_bytes=VMEM_LIMIT)


def _premix_kernel(*refs, dils, chunk, window):
    it = iter(refs)
    x_ref, nf_ref, wg_ref, wu_ref, wd_ref, nm_ref, win_ref = (next(it) for _ in range(7))
    perm = {d: next(it) for d in sorted(set(dils) | {chunk})}
    src_new = (next(it), next(it)) if window else None
    x1_ref, q_ref, k_ref, v_ref = (next(it) for _ in range(4))
    strided = {d: (next(it), next(it), next(it)) for d in dils}
    kf_ref, vf_ref, uf_ref, uc_ref = (next(it) for _ in range(4))
    if window:
        _background(_window_copies(*src_new, next(it), next(it)))

    x1 = _ffn(x_ref[...], nf_ref[...], wg_ref, wu_ref, wd_ref)
    x1_ref[...] = x1
    h = _rms(x1, nm_ref[...]).astype(BF16)
    qkvu = _dot(h, win_ref[...])
    q = (qkvu[:, :D_ATT] * (1.0 / math.sqrt(HEAD_DIM))).astype(BF16)
    kf = qkvu[:, D_ATT:2 * D_ATT]
    vf = qkvu[:, 2 * D_ATT:3 * D_ATT]
    uf = qkvu[:, 3 * D_ATT:]
    k, v, u = kf.astype(BF16), vf.astype(BF16), uf.astype(BF16)
    q_ref[...], k_ref[...], v_ref[...] = q, k, v
    kf_ref[...], vf_ref[...], uf_ref[...] = kf, vf, uf
    tm = x1.shape[0]
    for d, outs in strided.items():
        for val, out in zip((q, k, v), outs):
            out[0] = _dot(perm[d][...], val).astype(BF16).reshape(d, tm // d, D_ATT)
    up = _dot(perm[chunk][...], u).astype(BF16)
    rows = tm // chunk
    for t in range(chunk):
        for j in range(N_JT):
            uc_ref[j, :, t * LANE:(t + 1) * LANE] = up[t * rows:(t + 1) * rows, j * LANE:(j + 1) * LANE]


def _window_io(window, args, in_specs, out_specs, out_shape, scratch):
    if window is None:
        return
    src, new = window
    args += [src, new]
    in_specs += [_ANY, _ANY]
    out_specs.append(_ANY)
    out_shape.append(jax.ShapeDtypeStruct(src.shape, src.dtype))
    scratch.append(pltpu.SemaphoreType.DMA((src.shape[0] + 1,)))


def _premix(x, weights, perms, tm, seq, dils, chunk, window=None):
    m = x.shape[0]
    per_seq = seq // tm
    row = lambda w: pl.BlockSpec((tm, w), lambda i: (i, 0))
    consts = list(weights) + [perms[d] for d in sorted(set(dils) | {chunk})]
    in_specs = [row(D_MODEL)] + [_const_spec(c.shape) for c in consts]
    out_specs = [row(D_MODEL), row(D_ATT), row(D_ATT), row(D_ATT)]
    out_shape = [jax.ShapeDtypeStruct((m, D_MODEL), F32)] + [jax.ShapeDtypeStruct((m, D_ATT), BF16)] * 3
    for d in dils:
        out_specs += [pl.BlockSpec((1, d, tm // d, D_ATT), lambda i: (i // per_seq, 0, i % per_seq, 0))] * 3
        out_shape += [jax.ShapeDtypeStruct((m // seq, d, seq // d, D_ATT), BF16)] * 3
    out_specs += [row(D_ATT)] * 3 + [pl.BlockSpec((N_JT, tm // chunk, chunk * LANE), lambda i: (0, i, 0))]
    out_shape += [jax.ShapeDtypeStruct((m, D_ATT), F32)] * 3
    out_shape += [jax.ShapeDtypeStruct((N_JT, m // chunk, chunk * LANE), BF16)]
    args, scratch = [x] + consts, []
    _window_io(window, args, in_specs, out_specs, out_shape, scratch)
    return pl.pallas_call(
        functools.partial(_premix_kernel, dils=tuple(dils), chunk=chunk, window=window is not None),
        grid=(m // tm,),
        in_specs=in_specs, out_specs=out_specs, out_shape=out_shape, scratch_shapes=scratch,
        compiler_params=_params(1),
        name="premix",
    )(*args)


def _attn_prompt_kernel(q_ref, kc_ref, kp_ref, vc_ref, vp_ref, tab_ref, o_ref, l_ref, kbuf, vbuf, *, qb):
    first = pl.program_id(2) == 0
    kbuf[0:N_STEP, :] = kp_ref[0, 0]
    kbuf[N_STEP:, :] = kc_ref[0, 0]
    vbuf[0:N_STEP, :] = vp_ref[0, 0]
    vbuf[N_STEP:, :] = vc_ref[0, 0]
    nk = 2 * N_STEP
    lo_half = lax.broadcasted_iota(jnp.int32, (N_STEP, LANE), 1) < HEAD_DIM
    keep_lo = jnp.where(lo_half, 1.0, 0.0).astype(BF16)
    keep_hi = jnp.where(lo_half, 0.0, 1.0).astype(BF16)
    col = lax.broadcasted_iota(jnp.int32, (N_STEP, nk), 1)
    before_start = jnp.where((col < N_STEP) & first, NEG, 0.0)

    def block(qi, carry):
        r0 = pl.multiple_of(qi * N_STEP, N_STEP)
        edge = jnp.where(qi == 0, before_start, 0.0)
        for hp in range(N_HEADS // 2):
            lanes = slice(hp * LANE, (hp + 1) * LANE)
            qp = q_ref[0, 0, pl.ds(r0, N_STEP), lanes]
            kp = kbuf[pl.ds(r0, nk), lanes]
            vp = vbuf[pl.ds(r0, nk), lanes]
            o_pair, l_pair = None, None
            for hh in range(2):
                qm = qp * (keep_lo if hh == 0 else keep_hi)
                s = lax.dot_general(qm, kp, (((1,), (1,)), ((), ())), preferred_element_type=F32)
                s = s + tab_ref[2 * hp + hh] + edge
                m = jnp.max(s, axis=-1, keepdims=True)
                e = jnp.exp(s - m)
                den = jnp.sum(e, axis=-1, keepdims=True)
                o = _dot(e.astype(BF16), vp) / den
                lse = jnp.broadcast_to(m + jnp.log(den), (N_STEP, LANE))
                if hh == 0:
                    o_pair, l_pair = o, lse
                else:
                    o_pair = jnp.where(lo_half, o_pair, o)
                    l_pair = jnp.where(lo_half, l_pair, lse)
            o_ref[0, 0, pl.ds(r0, N_STEP), lanes] = o_pair
            l_ref[0, 0, pl.ds(r0, N_STEP), lanes] = l_pair
        return carry

    lax.fori_loop(0, qb // N_STEP, block, 0)


def _attn_prompt_branch(q, k, v, tab):
    batch, dil, sub, _ = q.shape
    qb = min(sub, 512)
    assert sub % qb == 0 and qb % N_STEP == 0
    per = qb // N_STEP
    cur = pl.BlockSpec((1, 1, qb, D_ATT), lambda b, r, i: (b, r, i, 0))
    prev = pl.BlockSpec((1, 1, N_STEP, D_ATT), lambda b, r, i: (b, r, jnp.maximum(i * per - 1, 0), 0))
    return pl.pallas_call(
        functools.partial(_attn_prompt_kernel, qb=qb),
        grid=(batch, dil, sub // qb),
        in_specs=[cur, cur, prev, cur, prev, _const_spec(tab.shape)],
        out_specs=[cur, cur],
        out_shape=[jax.ShapeDtypeStruct(q.shape, F32)] * 2,
        scratch_shapes=[pltpu.VMEM((qb + N_STEP, D_ATT), BF16)] * 2,
        compiler_params=_params(3),
        name=f"attn_prompt_d{dil}",
    )(q, k, k, v, v, tab)


Q_PAD = 8
NEW_SLOT = 16


def _sample_layout(cache_len, t_new):
    assert t_new <= Q_PAD and t_new <= NEW_SLOT
    (w1, d1), (w4, d4), (w16, d16) = BRANCHES
    assert d1 == 1 and cache_len % d16 == 0 and t_new <= d16 and cache_len >= w16
    win1 = w1
    win4 = w4
    n16 = cache_len // d16
    off4 = win1
    off16 = off4 + win4
    offn = off16 + t_new * n16
    total = -(-(offn + 3 * NEW_SLOT) // LANE) * LANE
    branch_of = np.full((t_new, total), -1, np.int32)
    step_of = np.zeros((t_new, total), np.int32)
    for i in range(t_new):
        col_of = [{}, {}, {}]
        for c in range(win1):
            col_of[0][cache_len - win1 + c] = c
        for c in range(win4):
            col_of[1][cache_len - win4 + c] = off4 + c
        for g in range(n16):
            col_of[2][g * d16 + i] = off16 + i * n16 + g
        for bi in range(3):
            for t in range(t_new):
                col_of[bi][cache_len + t] = offn + bi * NEW_SLOT + t
        for bi, (w, d) in enumerate(BRANCHES):
            for j in range(w // d + 1):
                ext = cache_len + i - j * d
                assert ext >= 0
                c = col_of[bi][ext]
                assert branch_of[i, c] == -1
                branch_of[i, c] = bi
                step_of[i, c] = j
    return dict(win1=win1, win4=win4, n16=n16, off4=off4, off16=off16, offn=offn, total=total,
                branch_of=branch_of, step_of=step_of)


def _attn_sample_kernel(q_ref, kn_ref, vn_ref, k1_ref, k4_ref, k16_ref, v1_ref, v4_ref, v16_ref, tab_ref,
                        o_ref, qs, kbuf, vbuf, *, t_new, lay):
    off4, off16, offn, n16, total = lay["off4"], lay["off16"], lay["offn"], lay["n16"], lay["total"]
    q = q_ref[0].astype(F32)
    qs[...] = jnp.zeros(qs.shape, F32)
    for buf, new_ref, w1_ref, w4_ref, w16_ref in ((kbuf, kn_ref, k1_ref, k4_ref, k16_ref),
                                                   (vbuf, vn_ref, v1_ref, v4_ref, v16_ref)):
        new = new_ref[0].astype(F32)
        for h in range(N_HEADS):
            buf[h, 0:off4, :] = w1_ref[0, :, h, :]
            buf[h, off4:off16, :] = w4_ref[0, :, h, :]
            for r in range(t_new):
                buf[h, off16 + r * n16:off16 + (r + 1) * n16, :] = w16_ref[0, :, r, h, :]
            buf[h, offn:, :] = jnp.zeros((total - offn, HEAD_DIM), F32)
            for bi in range(3):
                buf[h, offn + bi * NEW_SLOT:offn + bi * NEW_SLOT + t_new, :] = new[:, h * HEAD_DIM:(h + 1) * HEAD_DIM]
    for h in range(N_HEADS):
        qs[h, 0:t_new, :] = q[:, h * HEAD_DIM:(h + 1) * HEAD_DIM]
        s = lax.dot_general(qs[h].astype(BF16), kbuf[h].astype(BF16), (((1,), (1,)), ((), ())),
                            preferred_element_type=F32)
        s = s + tab_ref[h]
        m = jnp.max(s, axis=-1, keepdims=True)
        e = jnp.exp(s - m)
        den = jnp.sum(e, axis=-1, keepdims=True)
        o = _dot(e.astype(BF16), vbuf[h].astype(BF16)) / den
        o_ref[0, :, h * HEAD_DIM:(h + 1) * HEAD_DIM] = o[:t_new]


def _attn_sample(q, k_new, v_new, cache_k, cache_v, layer, tab, lay, batch, t_new):
    depth, _, cache_len = cache_k.shape[:3]
    d16 = BRANCHES[2][1]
    win1, win4, n16, total = lay["win1"], lay["win4"], lay["n16"], lay["total"]
    sq = pl.Squeezed()
    tok = pl.BlockSpec((1, t_new, D_ATT), lambda b: (b, 0, 0))
    s1 = pl.BlockSpec((sq, 1, win1, N_HEADS, HEAD_DIM), lambda b: (layer, b, cache_len // win1 - 1, 0, 0))
    s4 = pl.BlockSpec((sq, 1, win4, N_HEADS, HEAD_DIM), lambda b: (layer, b, cache_len // win4 - 1, 0, 0))
    s16 = pl.BlockSpec((sq, 1, n16, t_new, N_HEADS, HEAD_DIM), lambda b: (layer, b, 0, 0, 0, 0))
    tokv = lambda a: a.reshape(batch, t_new, D_ATT)
    strided = lambda a: a.reshape(depth, batch, n16, d16, N_HEADS, HEAD_DIM)
    out = pl.pallas_call(
        functools.partial(_attn_sample_kernel, t_new=t_new, lay=lay),
        grid=(batch,),
        in_specs=[tok, tok, tok, s1, s4, s16, s1, s4, s16, _const_spec(tab.shape)],
        out_specs=tok,
        out_shape=jax.ShapeDtypeStruct((batch, t_new, D_ATT), F32),
        scratch_shapes=[pltpu.VMEM((N_HEADS, Q_PAD, HEAD_DIM), F32),
                        pltpu.VMEM((N_HEADS, total, HEAD_DIM), F32),
                        pltpu.VMEM((N_HEADS, total, HEAD_DIM), F32)],
        compiler_params=_params(1),
        name="attn_sample",
    )(tokv(q), tokv(k_new), tokv(v_new), cache_k, cache_k, strided(cache_k),
      cache_v, cache_v, strided(cache_v), tab)
    return out.reshape(batch * t_new, D_ATT)


def _ssm_kernel(u_ref, h0_ref, w_ref, pw_ref, qw_ref, a_ref, y_ref, hl_ref, z_buf, hb_buf,
                *, chunk, scan_rows):
    n_pair = chunk // TOK_PER_PAIR
    ub = u_ref[0]
    z = _dot(ub, pw_ref[0])
    ar = a_ref[0, 0:1, :]
    ai = a_ref[0, 1:2, :]
    h0 = h0_ref[0, 0]
    if scan_rows:
        z_buf[...] = z

        def step(r, h):
            hr, hi = h
            hb_buf[pl.ds(r, 1), 0:PLANE] = hr
            hb_buf[pl.ds(r, 1), PLANE:] = hi
            zr = z_buf[pl.ds(r, 1), 0:PLANE]
            zi = z_buf[pl.ds(r, 1), PLANE:]
            return ar * hr - ai * hi + zr, ar * hi + ai * hr + zi

        hr, hi = lax.fori_loop(0, ub.shape[0], step, (h0[:, 0:PLANE], h0[:, PLANE:]))
        hl_ref[0, 0, :, 0:PLANE] = hr
        hl_ref[0, 0, :, PLANE:] = hi
        hb = hb_buf[...].astype(BF16)
    else:
        hr, hi = h0[:, 0:PLANE], h0[:, PLANE:]
        hl_ref[0, 0, :, 0:PLANE] = ar * hr - ai * hi + z[:, 0:PLANE]
        hl_ref[0, 0, :, PLANE:] = ar * hi + ai * hr + z[:, PLANE:]
        hb = h0.astype(BF16)
    for b in range(n_pair):
        acc = _dot(hb, qw_ref[0, b])
        for a in range(b + 1):
            acc = acc + _dot(ub[:, a * MXU:(a + 1) * MXU], w_ref[0, b - a])
        y_ref[0, :, b * MXU:(b + 1) * MXU] = acc


def _ssm(uc, h0, tabs, chunk, rows_per_block, scan_rows):
    w, pw, qw, a_pow = tabs
    rows, width = uc.shape[1:]
    n_blocks = rows // rows_per_block
    seqs = h0.shape[2]
    return pl.pallas_call(
        functools.partial(_ssm_kernel, chunk=chunk, scan_rows=scan_rows),
        grid=(N_JT, n_blocks),
        in_specs=[pl.BlockSpec((1, rows_per_block, width), lambda j, b: (j, b, 0)),
                  pl.BlockSpec((1, 1, seqs, 2 * PLANE), lambda j, b: (j, b, 0, 0)),
                  pl.BlockSpec((1,) + w.shape[1:], lambda j, b: (j, 0, 0, 0)),
                  pl.BlockSpec((1,) + pw.shape[1:], lambda j, b: (j, 0, 0)),
                  pl.BlockSpec((1,) + qw.shape[1:], lambda j, b: (j, 0, 0, 0)),
                  pl.BlockSpec((1,) + a_pow.shape[1:], lambda j, b: (j, 0, 0))],
        out_specs=[pl.BlockSpec((1, rows_per_block, width), lambda j, b: (j, b, 0)),
                   pl.BlockSpec((1, 1, seqs, 2 * PLANE), lambda j, b: (j, b, 0, 0))],
        out_shape=[jax.ShapeDtypeStruct(uc.shape, F32),
                   jax.ShapeDtypeStruct(h0.shape, F32)],
        scratch_shapes=[pltpu.VMEM((rows_per_block, 2 * PLANE), F32)] * 2,
        compiler_params=_params(2),
        name=f"ssm_chunk{chunk}",
    )(uc, h0, w, pw, qw, a_pow)


def _ssm_tables(a_re, a_im, log_dt, b_re, b_im, c_re, c_im, chunk):
    hi = lax.Precision.HIGHEST
    dt = jnp.exp(log_dt)[:, None]
    den = a_re * a_re + a_im * a_im
    er, ang = jnp.exp(a_re * dt), a_im * dt
    ab_re, ab_im = er * jnp.cos(ang), er * jnp.sin(ang)
    nr, ni = ab_re - 1.0, ab_im
    f_re, f_im = (nr * a_re + ni * a_im) / den, (ni * a_re - nr * a_im) / den
    bb_re = f_re[..., None] * b_re - f_im[..., None] * b_im
    bb_im = f_re[..., None] * b_im + f_im[..., None] * b_re

    def power(taus):
        taus = jnp.asarray(taus, F32)[:, None, None]
        mag, ang_t = jnp.exp(a_re * dt * taus), a_im * dt * taus
        return mag * jnp.cos(ang_t), mag * jnp.sin(ang_t)

    p_re, p_im = power(np.arange(chunk + 1))
    pr, pi = (x[..., None] for x in power(chunk - 1 - np.arange(chunk)))
    s_re, s_im = pr * bb_re - pi * bb_im, pr * bb_im + pi * bb_re
    qr, qi = p_re[1:, :, None, :], p_im[1:, :, None, :]
    e_re, e_im = c_re * qr - c_im * qi, c_re * qi + c_im * qr
    lr, li = p_re[:chunk, :, None, :], p_im[:chunk, :, None, :]
    cl_re, cl_im = c_re * lr - c_im * li, c_re * li + c_im * lr
    kern = (jnp.einsum("tgon,gni->tgoi", cl_re, bb_re, precision=hi)
            - jnp.einsum("tgon,gni->tgoi", cl_im, bb_im, precision=hi))

    eye = jnp.eye(GROUPS_PER_TILE, dtype=F32)
    n_pair = chunk // TOK_PER_PAIR
    tile = lambda x: x.reshape((x.shape[0], N_JT, GROUPS_PER_TILE) + x.shape[2:])
    kern_t = tile(kern)
    kern_t = jnp.concatenate([kern_t, jnp.zeros_like(kern_t[:1])], axis=0)
    lag = (TOK_PER_PAIR * np.arange(n_pair)[:, None, None]
           + np.arange(TOK_PER_PAIR)[None, None, :] - np.arange(TOK_PER_PAIR)[None, :, None])
    kl = kern_t[lag]
    w = jnp.einsum("dpqjgoi,gh->jdpgiqho", kl, eye)
    w = w.reshape(N_JT, n_pair, MXU, MXU)
    s_t = jnp.stack([tile(s_re), tile(s_im)], axis=0)
    pw = jnp.einsum("ptjgni,gh->jtgiphn", s_t, eye)
    pw = pw.reshape(N_JT, chunk * LANE, 2 * PLANE)
    e_t = jnp.stack([tile(e_re), -tile(e_im)], axis=0)
    qw = jnp.einsum("ptjgon,gh->jpgntho", e_t, eye)
    qw = qw.reshape(N_JT, 2 * PLANE, n_pair, MXU).transpose(0, 2, 1, 3)
    a_pow = jnp.stack([p_re[chunk].reshape(N_JT, PLANE), p_im[chunk].reshape(N_JT, PLANE)], axis=1)
    return w.astype(BF16), pw.astype(BF16), qw.astype(BF16), a_pow


def _state_to_tiles(re, im, n_blocks):
    n_seq = re.shape[0]
    t = lambda x: x.reshape(n_blocks, n_seq // n_blocks, N_JT, PLANE).transpose(2, 0, 1, 3)
    return jnp.concatenate([t(re), t(im)], axis=-1)


def _tiles_to_state(h):
    n_seq = h.shape[1] * h.shape[2]
    t = lambda x: x.transpose(1, 2, 0, 3).reshape(n_seq, N_GROUPS, SSM_STATE)
    return t(h[..., :PLANE]), t(h[..., PLANE:])


def _postmix_kernel(*refs, dils, chunk, final, window):
    it = iter(refs)
    x1_ref = next(it)
    att_refs = [(next(it), next(it)) for _ in dils] if len(dils) > 1 else [(next(it), None)]
    y_ref, uf_ref, p_ref = next(it), next(it), next(it)
    (natt_ref, nssm_ref, dskip_ref, wglu_ref, bglu_ref, wout_ref, nf_ref, wg_ref, wu_ref, wd_ref,
     nple_ref, wpg_ref, wpp_ref) = (next(it) for _ in range(13))
    perm_t = {d: next(it) for d in sorted((set(dils) | {chunk}) - {1})}
    nfin_ref = next(it) if final else None
    src_new = (next(it), next(it)) if window else None
    out_ref = next(it)
    dst_ref = next(it) if window else None
    ybuf = next(it)
    if window:
        _background(_window_copies(*src_new, dst_ref, next(it)))
    tm = x1_ref.shape[0]

    def natural(ref, d):
        val = ref[...] if d == 1 else ref[0].reshape(tm, D_ATT)
        return val if d == 1 else _permute_f32(perm_t[d], val)

    if len(dils) == 1:
        att = att_refs[0][0][...]
    else:
        outs = [natural(o, d) for d, (o, _) in zip(dils, att_refs)]
        lses = [natural(l, d) for d, (_, l) in zip(dils, att_refs)]
        top = functools.reduce(jnp.maximum, lses)
        ws = [jnp.exp(l - top) for l in lses]
        att = sum(w * o for w, o in zip(ws, outs)) / sum(ws)
    a_n = _rms(att, natt_ref[...])
    rows = tm // chunk
    for t in range(chunk):
        for j in range(N_JT):
            ybuf[t * rows:(t + 1) * rows, j * LANE:(j + 1) * LANE] = y_ref[j, :, t * LANE:(t + 1) * LANE]
    y = _permute_f32(perm_t[chunk], ybuf[...]) + dskip_ref[...] * uf_ref[...]
    z = jax.nn.gelu(y)
    s = z * jax.nn.sigmoid(_dot(z.astype(BF16), wglu_ref[...]) + bglu_ref[...])
    s_n = _rms(s, nssm_ref[...])
    mixed = jnp.concatenate([a_n, s_n], axis=-1).astype(BF16)
    x2 = x1_ref[...] + _dot(mixed, wout_ref[...])
    x3 = _ffn(x2, nf_ref[...], wg_ref, wu_ref, wd_ref)
    gate = jax.nn.sigmoid(_dot(_rms(x3, nple_ref[...]).astype(BF16), wpg_ref[...]))
    x4 = x3 + gate * _dot(p_ref[...].astype(BF16), wpp_ref[...])
    out_ref[...] = _rms(x4, nfin_ref[...]) if final else x4


def _postmix(x1, att_planes, dils, y, uf, p, weights, perms_t, nfin, tm, seq, chunk, window=None):
    m = x1.shape[0]
    per_seq = seq // tm
    final = nfin is not None
    row = lambda w: pl.BlockSpec((tm, w), lambda i: (i, 0))
    att_args, att_specs = [], []
    for d, (o, l) in zip(dils, att_planes):
        spec = row(D_ATT) if d == 1 else pl.BlockSpec((1, d, tm // d, D_ATT),
                                                       lambda i: (i // per_seq, 0, i % per_seq, 0))
        for a in ((o, l) if len(dils) > 1 else (o,)):
            att_args.append(a.reshape(m, D_ATT) if d == 1 else a)
            att_specs.append(spec)
    consts = list(weights) + [perms_t[d] for d in sorted((set(dils) | {chunk}) - {1})] + ([nfin] if final else [])
    args = [x1] + att_args + [y, uf, p] + consts
    in_specs = ([row(D_MODEL)] + att_specs
                + [pl.BlockSpec((N_JT, tm // chunk, chunk * LANE), lambda i: (0, i, 0)), row(D_SSM), row(D_PLE)]
                + [_const_spec(c.shape) for c in consts])
    out_specs, out_shape = [row(D_MODEL)], [jax.ShapeDtypeStruct((m, D_MODEL), F32)]
    scratch = [pltpu.VMEM((tm, D_SSM), F32)]
    _window_io(window, args, in_specs, out_specs, out_shape, scratch)
    return pl.pallas_call(
        functools.partial(_postmix_kernel, dils=tuple(dils), chunk=chunk, final=final,
                          window=window is not None),
        grid=(m // tm,),
        in_specs=in_specs, out_specs=out_specs, out_shape=out_shape, scratch_shapes=scratch,
        compiler_params=_params(1),
        name="postmix",
    )(*args)


def _t5_bucket(dist):
    dist = np.asarray(dist, dtype=np.int64)
    exact = N_BUCKETS // 2
    ratio = np.log(np.maximum(dist, 1) / exact) / np.log(MAX_DISTANCE / exact)
    large = np.minimum(exact + (ratio * (N_BUCKETS - exact)).astype(np.int64), N_BUCKETS - 1)
    return np.where(dist < exact, dist, large).astype(np.int32)


def _branch_bias(rel_bias, dil):
    return rel_bias[_t5_bucket(np.arange(N_STEP + 1) * dil)].T.astype(F32)


def _prompt_table(bias):
    step = np.arange(N_STEP)[:, None] - np.arange(2 * N_STEP)[None, :] + N_STEP
    valid = (step >= 0) & (step <= N_STEP)
    return jnp.where(valid[None], bias[:, np.clip(step, 0, N_STEP)], NEG)


def _sample_table(biases, lay, t_new):
    total = lay["total"]
    branch_of = np.zeros((Q_PAD, total), np.int32) - 1
    step_of = np.zeros((Q_PAD, total), np.int32)
    branch_of[:t_new], step_of[:t_new] = lay["branch_of"], lay["step_of"]
    stacked = jnp.stack(biases, axis=0)
    vals = stacked[np.maximum(branch_of, 0), :, step_of]
    tab = jnp.where((branch_of >= 0)[..., None], vals, NEG)
    tab = tab.at[t_new:].set(0.0)
    return tab.transpose(2, 0, 1)


def kernel(x_prompt, x_sample, p_prompt, p_sample, cache_k, cache_v, state_ssm_re, state_ssm_im,
           rel_bias, w_in, w_out, norm_mix, norm_att_out, norm_ssm_out, norm_ffn, ffn_w_gate,
           ffn_w_up, ffn_w_down, ssm_a_re, ssm_a_im, ssm_log_dt, ssm_b_re, ssm_b_im, ssm_c_re,
           ssm_c_im, ssm_d, w_glu, b_glu, norm_ple, w_ple_gate, w_ple_proj, norm_final):
    depth = w_in.shape[0]
    batch, seq, _ = x_prompt.shape
    dec_batch, dec_seq, _ = x_sample.shape
    cache_len = cache_k.shape[2]
    keep = min(BRANCHES[-1][0], seq)
    dils = tuple(d for _, d in BRANCHES)
    chunk_p = dils[-1]
    tm_p = 256
    tm_s = dec_batch * dec_seq
    assert seq % tm_p == 0 and chunk_p in dils and dec_seq % TOK_PER_PAIR == 0 and tm_s % 8 == 0

    xp = x_prompt.reshape(batch * seq, D_MODEL)
    xs = x_sample.reshape(tm_s, D_MODEL)
    lay = _sample_layout(cache_len, dec_seq)
    biases = [_branch_bias(rel_bias, d) for d in dils]
    tabs_p = [_prompt_table(b) for b in biases]
    tab_s = _sample_table(biases, lay, dec_seq)
    perms_p = {d: jnp.asarray(_perm_matrix(tm_p, d), BF16) for d in dils if d > 1}
    perms_pt = {d: jnp.asarray(_perm_matrix(tm_p, d).T, BF16) for d in dils if d > 1}
    perms_s = {dec_seq: jnp.asarray(_perm_matrix(tm_s, dec_seq), BF16)}
    perms_st = {dec_seq: jnp.asarray(_perm_matrix(tm_s, dec_seq).T, BF16)}
    row = lambda v: v.reshape(1, -1)
    bf = lambda a: a.astype(BF16)

    def layer_weights(i):
        pre = (row(norm_ffn[i, 0]), bf(ffn_w_gate[i, 0]), bf(ffn_w_up[i, 0]), bf(ffn_w_down[i, 0]),
               row(norm_mix[i]), bf(w_in[i]))
        post = (row(norm_att_out[i]), row(norm_ssm_out[i]), row(ssm_d[i]), bf(w_glu[i]), row(b_glu[i]),
                bf(w_out[i]), row(norm_ffn[i, 1]), bf(ffn_w_gate[i, 1]), bf(ffn_w_up[i, 1]),
                bf(ffn_w_down[i, 1]), row(norm_ple[i]), bf(w_ple_gate[i]), bf(w_ple_proj[i]))
        nfin = row(norm_final) if i == depth - 1 else None
        ssm_par = (ssm_a_re[i], ssm_a_im[i], ssm_log_dt[i], ssm_b_re[i], ssm_b_im[i], ssm_c_re[i], ssm_c_im[i])
        return pre, post, nfin, ssm_par

    k_news, v_news, rss, iss = [], [], [], []
    for i in range(depth):
        pre, post, nfin, ssm_par = layer_weights(i)
        x1, q, k, v, kf, vf, uf, uc = _premix(xs, pre, perms_s, tm_s, tm_s, (), dec_seq)
        att = _attn_sample(q, k, v, cache_k, cache_v, i, tab_s, lay, dec_batch, dec_seq)
        h0 = _state_to_tiles(state_ssm_re[i], state_ssm_im[i], 1)
        y, h_last = _ssm(uc, h0, _ssm_tables(*ssm_par, dec_seq), dec_seq, dec_batch, False)
        xs = _postmix(x1, [(att, None)], (1,), y, uf, p_sample[i].reshape(tm_s, D_PLE), post, perms_st, nfin,
                      tm_s, tm_s, dec_seq)[0]
        k_news.append(kf.reshape(dec_batch, dec_seq, N_HEADS, HEAD_DIM))
        v_news.append(vf.reshape(dec_batch, dec_seq, N_HEADS, HEAD_DIM))
        re, im = _tiles_to_state(h_last)
        rss.append(re)
        iss.append(im)

    kp, vp, rp, ip = [], [], [], []
    k_sample = v_sample = None
    for i in range(depth):
        pre, post, nfin, ssm_par = layer_weights(i)
        outs = list(_premix(xp, pre, perms_p, tm_p, seq, dils[1:], chunk_p,
                            window=(cache_k, jnp.stack(k_news)) if i == 0 else None))
        if i == 0:
            k_sample = outs.pop()
        x1, q, k, v = outs[:4]
        kf, vf, uf, uc = outs[-4:]
        nat = lambda a: a.reshape(batch, 1, seq, D_ATT)
        qkv = [(nat(q), nat(k), nat(v))] + [tuple(outs[4 + 3 * n:7 + 3 * n]) for n in range(len(dils) - 1)]
        planes = [_attn_prompt_branch(*qkv[bi], tabs_p[bi]) for bi in range(len(dils))]
        zeros = jnp.zeros((batch, N_GROUPS, SSM_STATE), F32)
        y, h_last = _ssm(uc, _state_to_tiles(zeros, zeros, batch), _ssm_tables(*ssm_par, chunk_p),
                         chunk_p, seq // chunk_p, True)
        outs = list(_postmix(x1, planes, dils, y, uf, p_prompt[i].reshape(batch * seq, D_PLE), post, perms_pt,
                             nfin, tm_p, seq, chunk_p,
                             window=(cache_v, jnp.stack(v_news)) if i == 0 else None))
        if i == 0:
            v_sample = outs.pop()
        xp = outs[0]
        kp.append(kf.reshape(batch, seq, N_HEADS, HEAD_DIM)[:, -keep:])
        vp.append(vf.reshape(batch, seq, N_HEADS, HEAD_DIM)[:, -keep:])
        re, im = _tiles_to_state(h_last)
        rp.append(re)
        ip.append(im)

    return (xp.reshape(batch, seq, D_MODEL), xs.reshape(dec_batch, dec_seq, D_MODEL),
            jnp.stack(kp), jnp.stack(vp), jnp.stack(rp), jnp.stack(ip),
            k_sample, v_sample, jnp.stack(rss), jnp.stack(iss))
```

```python
import functools
import math

import numpy as np
import jax
import jax.numpy as jnp
from jax import lax
from jax.experimental import pallas as pl
from jax.experimental.pallas import tpu as pltpu

F32 = jnp.float32
BF16 = jnp.bfloat16

D_MODEL = 1024
D_ATT = 512
D_SSM = 512
HEAD_DIM = 64
N_HEADS = 8
SSM_GROUP = 16
N_GROUPS = 32
SSM_STATE = 64
BRANCHES = ((128, 1), (512, 4), (2048, 16))
N_STEP = 128
N_BUCKETS = 32
MAX_DISTANCE = 2048
D_FF = 2816
D_PLE = 256
D_IN = 3 * D_ATT + D_SSM
EPS = 1e-6
NEG = -1e30

LANE = 128
MXU = 256
GROUPS_PER_TILE = LANE // SSM_GROUP
N_JT = D_SSM // LANE
PLANE = GROUPS_PER_TILE * SSM_STATE
TOK_PER_PAIR = MXU // LANE
FF_CHUNK = 256
VMEM_LIMIT = 56 * 1024 * 1024

assert all(w // d == N_STEP for w, d in BRANCHES)
assert D_FF % FF_CHUNK == 0


def _rms(x, g):
    return x * lax.rsqrt(jnp.mean(x * x, axis=-1, keepdims=True) + EPS) * g


def _dot(a, b):
    return jnp.dot(a, b, preferred_element_type=F32)


def _ffn(x, g, wg_ref, wu_ref, wd_ref):
    h = _rms(x, g).astype(BF16)
    acc = jnp.zeros(x.shape, F32)
    for c in range(D_FF // FF_CHUNK):
        sl = slice(c * FF_CHUNK, (c + 1) * FF_CHUNK)
        gate = _dot(h, wg_ref[:, sl])
        up = _dot(h, wu_ref[:, sl])
        act = (jax.nn.silu(gate) * up).astype(BF16)
        acc = acc + _dot(act, wd_ref[sl, :])
    return x + 0.5 * acc


def _perm_matrix(n, d):
    p = np.zeros((n, n), np.float32)
    s, r = np.meshgrid(np.arange(n // d), np.arange(d), indexing="ij")
    p[(r * (n // d) + s).ravel(), (d * s + r).ravel()] = 1.0
    return p


def _permute_f32(pt_ref, x):
    hi = x.astype(BF16)
    rest = x - hi.astype(F32)
    mid = rest.astype(BF16)
    lo = (rest - mid.astype(F32)).astype(BF16)
    pt = pt_ref[...]
    return _dot(pt, hi) + _dot(pt, mid) + _dot(pt, lo)


def _const_spec(shape):
    nd = len(shape)
    return pl.BlockSpec(shape, lambda *_: (0,) * nd, pipeline_mode=pl.Buffered(1))


def _params(n_axes):
    return pltpu.CompilerParams(dimension_semantics=("arbitrary",) * n_axes,
                                vmem_limit_bytes=VMEM_LIMIT)


def _premix_kernel(*refs, dils, chunk):
    it = iter(refs)
    x_ref, nf_ref, wg_ref, wu_ref, wd_ref, nm_ref, win_ref = (next(it) for _ in range(7))
    perm = {d: next(it) for d in sorted(set(dils) | {chunk})}
    x1_ref, q_ref, k_ref, v_ref = (next(it) for _ in range(4))
    strided = {d: (next(it), next(it), next(it)) for d in dils}
    kf_ref, vf_ref, uf_ref, uc_ref = (next(it) for _ in range(4))

    x1 = _ffn(x_ref[...], nf_ref[...], wg_ref, wu_ref, wd_ref)
    x1_ref[...] = x1
    h = _rms(x1, nm_ref[...]).astype(BF16)
    qkvu = _dot(h, win_ref[...])
    q = (qkvu[:, :D_ATT] * (1.0 / math.sqrt(HEAD_DIM))).astype(BF16)
    kf = qkvu[:, D_ATT:2 * D_ATT]
    vf = qkvu[:, 2 * D_ATT:3 * D_ATT]
    uf = qkvu[:, 3 * D_ATT:]
    k, v, u = kf.astype(BF16), vf.astype(BF16), uf.astype(BF16)
    q_ref[...], k_ref[...], v_ref[...] = q, k, v
    uf_ref[...] = uf
    for h in range(N_HEADS):
        kf_ref[0, :, h, :] = kf[:, h * HEAD_DIM:(h + 1) * HEAD_DIM]
        vf_ref[0, :, h, :] = vf[:, h * HEAD_DIM:(h + 1) * HEAD_DIM]
    tm = x1.shape[0]
    for d, outs in strided.items():
        for val, out in zip((q, k, v), outs):
            out[0] = _dot(perm[d][...], val).astype(BF16).reshape(d, tm // d, D_ATT)
    up = _dot(perm[chunk][...], u).astype(BF16)
    rows = tm // chunk
    for t in range(chunk):
        for j in range(N_JT):
            uc_ref[j, :, t * LANE:(t + 1) * LANE] = up[t * rows:(t + 1) * rows, j * LANE:(j + 1) * LANE]


def _premix(x, weights, perms, tm, seq, keep, dils, chunk):
    m = x.shape[0]
    per_seq = seq // tm
    skip = (seq - keep) // tm
    assert keep % tm == 0
    kept = pl.BlockSpec((1, tm, N_HEADS, HEAD_DIM),
                        lambda i: (i // per_seq, jnp.maximum(i % per_seq - skip, 0), 0, 0))
    kept_shape = jax.ShapeDtypeStruct((m // seq, keep, N_HEADS, HEAD_DIM), F32)
    row = lambda w: pl.BlockSpec((tm, w), lambda i: (i, 0))
    consts = list(weights) + [perms[d] for d in sorted(set(dils) | {chunk})]
    in_specs = [row(D_MODEL)] + [_const_spec(c.shape) for c in consts]
    out_specs = [row(D_MODEL), row(D_ATT), row(D_ATT), row(D_ATT)]
    out_shape = [jax.ShapeDtypeStruct((m, D_MODEL), F32)] + [jax.ShapeDtypeStruct((m, D_ATT), BF16)] * 3
    for d in dils:
        out_specs += [pl.BlockSpec((1, d, tm // d, D_ATT), lambda i: (i // per_seq, 0, i % per_seq, 0))] * 3
        out_shape += [jax.ShapeDtypeStruct((m // seq, d, seq // d, D_ATT), BF16)] * 3
    out_specs += [kept, kept, row(D_SSM), pl.BlockSpec((N_JT, tm // chunk, chunk * LANE), lambda i: (0, i, 0))]
    out_shape += [kept_shape, kept_shape, jax.ShapeDtypeStruct((m, D_SSM), F32),
                  jax.ShapeDtypeStruct((N_JT, m // chunk, chunk * LANE), BF16)]
    return pl.pallas_call(
        functools.partial(_premix_kernel, dils=tuple(dils), chunk=chunk),
        grid=(m // tm,),
        in_specs=in_specs, out_specs=out_specs, out_shape=out_shape,
        compiler_params=_params(1),
        name="premix",
    )(x, *consts)


def _attn_prompt_kernel(q_ref, kc_ref, kp_ref, vc_ref, vp_ref, tab_ref, o_ref, l_ref, kbuf, vbuf, *, qb):
    first = pl.program_id(2) == 0
    kbuf[0:N_STEP, :] = kp_ref[0, 0]
    kbuf[N_STEP:, :] = kc_ref[0, 0]
    vbuf[0:N_STEP, :] = vp_ref[0, 0]
    vbuf[N_STEP:, :] = vc_ref[0, 0]
    nk = 2 * N_STEP
    lo_half = lax.broadcasted_iota(jnp.int32, (N_STEP, LANE), 1) < HEAD_DIM
    keep_lo = jnp.where(lo_half, 1.0, 0.0).astype(BF16)
    keep_hi = jnp.where(lo_half, 0.0, 1.0).astype(BF16)
    col = lax.broadcasted_iota(jnp.int32, (N_STEP, nk), 1)
    before_start = jnp.where((col < N_STEP) & first, NEG, 0.0)

    def block(qi, carry):
        r0 = pl.multiple_of(qi * N_STEP, N_STEP)
        edge = jnp.where(qi == 0, before_start, 0.0)
        for hp in range(N_HEADS // 2):
            lanes = slice(hp * LANE, (hp + 1) * LANE)
            qp = q_ref[0, 0, pl.ds(r0, N_STEP), lanes]
            kp = kbuf[pl.ds(r0, nk), lanes]
            vp = vbuf[pl.ds(r0, nk), lanes]
            o_pair, l_pair = None, None
            for hh in range(2):
                qm = qp * (keep_lo if hh == 0 else keep_hi)
                s = lax.dot_general(qm, kp, (((1,), (1,)), ((), ())), preferred_element_type=F32)
                s = s + tab_ref[2 * hp + hh] + edge
                m = jnp.max(s, axis=-1, keepdims=True)
                e = jnp.exp(s - m)
                den = jnp.sum(e, axis=-1, keepdims=True)
                o = _dot(e.astype(BF16), vp) / den
                lse = jnp.broadcast_to(m + jnp.log(den), (N_STEP, LANE))
                if hh == 0:
                    o_pair, l_pair = o, lse
                else:
                    o_pair = jnp.where(lo_half, o_pair, o)
                    l_pair = jnp.where(lo_half, l_pair, lse)
            o_ref[0, 0, pl.ds(r0, N_STEP), lanes] = o_pair
            l_ref[0, 0, pl.ds(r0, N_STEP), lanes] = l_pair
        return carry

    lax.fori_loop(0, qb // N_STEP, block, 0)


def _attn_prompt_branch(q, k, v, tab):
    batch, dil, sub, _ = q.shape
    qb = min(sub, 512)
    assert sub % qb == 0 and qb % N_STEP == 0
    per = qb // N_STEP
    cur = pl.BlockSpec((1, 1, qb, D_ATT), lambda b, r, i: (b, r, i, 0))
    prev = pl.BlockSpec((1, 1, N_STEP, D_ATT), lambda b, r, i: (b, r, jnp.maximum(i * per - 1, 0), 0))
    return pl.pallas_call(
        functools.partial(_attn_prompt_kernel, qb=qb),
        grid=(batch, dil, sub // qb),
        in_specs=[cur, cur, prev, cur, prev, _const_spec(tab.shape)],
        out_specs=[cur, cur],
        out_shape=[jax.ShapeDtypeStruct(q.shape, F32)] * 2,
        scratch_shapes=[pltpu.VMEM((qb + N_STEP, D_ATT), BF16)] * 2,
        compiler_params=_params(3),
        name=f"attn_prompt_d{dil}",
    )(q, k, k, v, v, tab)


Q_PAD = 8
NEW_SLOT = 16


def _sample_layout(cache_len, t_new):
    assert t_new <= Q_PAD and t_new <= NEW_SLOT
    (w1, d1), (w4, d4), (w16, d16) = BRANCHES
    assert d1 == 1 and cache_len % d16 == 0 and t_new <= d16 and cache_len >= w16
    win1 = w1
    win4 = w4
    n16 = cache_len // d16
    off4 = win1
    off16 = off4 + win4
    offn = off16 + t_new * n16
    total = -(-(offn + 3 * NEW_SLOT) // LANE) * LANE
    branch_of = np.full((t_new, total), -1, np.int32)
    step_of = np.zeros((t_new, total), np.int32)
    for i in range(t_new):
        col_of = [{}, {}, {}]
        for c in range(win1):
            col_of[0][cache_len - win1 + c] = c
        for c in range(win4):
            col_of[1][cache_len - win4 + c] = off4 + c
        for g in range(n16):
            col_of[2][g * d16 + i] = off16 + i * n16 + g
        for bi in range(3):
            for t in range(t_new):
                col_of[bi][cache_len + t] = offn + bi * NEW_SLOT + t
        for bi, (w, d) in enumerate(BRANCHES):
            for j in range(w // d + 1):
                ext = cache_len + i - j * d
                assert ext >= 0
                c = col_of[bi][ext]
                assert branch_of[i, c] == -1
                branch_of[i, c] = bi
                step_of[i, c] = j
    return dict(win1=win1, win4=win4, n16=n16, off4=off4, off16=off16, offn=offn, total=total,
                branch_of=branch_of, step_of=step_of)


def _attn_sample_kernel(q_ref, kn_ref, vn_ref, k1_ref, k4_ref, k16_ref, v1_ref, v4_ref, v16_ref, tab_ref,
                        o_ref, qs, kbuf, vbuf, *, t_new, lay):
    off4, off16, offn, n16, total = lay["off4"], lay["off16"], lay["offn"], lay["n16"], lay["total"]
    q = q_ref[0].astype(F32)
    qs[...] = jnp.zeros(qs.shape, F32)
    for buf, new_ref, w1_ref, w4_ref, w16_ref in ((kbuf, kn_ref, k1_ref, k4_ref, k16_ref),
                                                   (vbuf, vn_ref, v1_ref, v4_ref, v16_ref)):
        new = new_ref[0].astype(F32)
        for h in range(N_HEADS):
            buf[h, 0:off4, :] = w1_ref[0, :, h, :]
            buf[h, off4:off16, :] = w4_ref[0, :, h, :]
            for r in range(t_new):
                buf[h, off16 + r * n16:off16 + (r + 1) * n16, :] = w16_ref[0, :, r, h, :]
            buf[h, offn:, :] = jnp.zeros((total - offn, HEAD_DIM), F32)
            for bi in range(3):
                buf[h, offn + bi * NEW_SLOT:offn + bi * NEW_SLOT + t_new, :] = new[:, h * HEAD_DIM:(h + 1) * HEAD_DIM]
    for h in range(N_HEADS):
        qs[h, 0:t_new, :] = q[:, h * HEAD_DIM:(h + 1) * HEAD_DIM]
        s = lax.dot_general(qs[h].astype(BF16), kbuf[h].astype(BF16), (((1,), (1,)), ((), ())),
                            preferred_element_type=F32)
        s = s + tab_ref[h]
        m = jnp.max(s, axis=-1, keepdims=True)
        e = jnp.exp(s - m)
        den = jnp.sum(e, axis=-1, keepdims=True)
        o = _dot(e.astype(BF16), vbuf[h].astype(BF16)) / den
        o_ref[0, :, h * HEAD_DIM:(h + 1) * HEAD_DIM] = o[:t_new]


def _attn_sample(q, k_new, v_new, cache_k, cache_v, layer, tab, lay, batch, t_new):
    depth, _, cache_len = cache_k.shape[:3]
    d16 = BRANCHES[2][1]
    win1, win4, n16, total = lay["win1"], lay["win4"], lay["n16"], lay["total"]
    sq = pl.Squeezed()
    tok = pl.BlockSpec((1, t_new, D_ATT), lambda b: (b, 0, 0))
    s1 = pl.BlockSpec((sq, 1, win1, N_HEADS, HEAD_DIM), lambda b: (layer, b, cache_len // win1 - 1, 0, 0))
    s4 = pl.BlockSpec((sq, 1, win4, N_HEADS, HEAD_DIM), lambda b: (layer, b, cache_len // win4 - 1, 0, 0))
    s16 = pl.BlockSpec((sq, 1, n16, t_new, N_HEADS, HEAD_DIM), lambda b: (layer, b, 0, 0, 0, 0))
    tokv = lambda a: a.reshape(batch, t_new, D_ATT)
    strided = lambda a: a.reshape(depth, batch, n16, d16, N_HEADS, HEAD_DIM)
    out = pl.pallas_call(
        functools.partial(_attn_sample_kernel, t_new=t_new, lay=lay),
        grid=(batch,),
        in_specs=[tok, tok, tok, s1, s4, s16, s1, s4, s16, _const_spec(tab.shape)],
        out_specs=tok,
        out_shape=jax.ShapeDtypeStruct((batch, t_new, D_ATT), F32),
        scratch_shapes=[pltpu.VMEM((N_HEADS, Q_PAD, HEAD_DIM), F32),
                        pltpu.VMEM((N_HEADS, total, HEAD_DIM), F32),
                        pltpu.VMEM((N_HEADS, total, HEAD_DIM), F32)],
        compiler_params=_params(1),
        name="attn_sample",
    )(tokv(q), tokv(k_new), tokv(v_new), cache_k, cache_k, strided(cache_k),
      cache_v, cache_v, strided(cache_v), tab)
    return out.reshape(batch * t_new, D_ATT)


def _ssm_kernel(u_ref, h0_ref, w_ref, pw_ref, qw_ref, a_ref, y_ref, hl_ref, z_buf, hb_buf,
                *, chunk, scan_rows):
    n_pair = chunk // TOK_PER_PAIR
    ub = u_ref[0]
    z = _dot(ub, pw_ref[0])
    ar = a_ref[0, 0:1, :]
    ai = a_ref[0, 1:2, :]
    h0 = h0_ref[0, 0]
    if scan_rows:
        z_buf[...] = z

        def step(r, h):
            hr, hi = h
            hb_buf[pl.ds(r, 1), 0:PLANE] = hr
            hb_buf[pl.ds(r, 1), PLANE:] = hi
            zr = z_buf[pl.ds(r, 1), 0:PLANE]
            zi = z_buf[pl.ds(r, 1), PLANE:]
            return ar * hr - ai * hi + zr, ar * hi + ai * hr + zi

        hr, hi = lax.fori_loop(0, ub.shape[0], step, (h0[:, 0:PLANE], h0[:, PLANE:]))
        hl_ref[0, 0, :, 0:PLANE] = hr
        hl_ref[0, 0, :, PLANE:] = hi
        hb = hb_buf[...].astype(BF16)
    else:
        hr, hi = h0[:, 0:PLANE], h0[:, PLANE:]
        hl_ref[0, 0, :, 0:PLANE] = ar * hr - ai * hi + z[:, 0:PLANE]
        hl_ref[0, 0, :, PLANE:] = ar * hi + ai * hr + z[:, PLANE:]
        hb = h0.astype(BF16)
    for b in range(n_pair):
        acc = _dot(hb, qw_ref[0, b])
        for a in range(b + 1):
            acc = acc + _dot(ub[:, a * MXU:(a + 1) * MXU], w_ref[0, b - a])
        y_ref[0, :, b * MXU:(b + 1) * MXU] = acc


def _ssm(uc, h0, tabs, chunk, rows_per_block, scan_rows):
    w, pw, qw, a_pow = tabs
    rows, width = uc.shape[1:]
    n_blocks = rows // rows_per_block
    seqs = h0.shape[2]
    return pl.pallas_call(
        functools.partial(_ssm_kernel, chunk=chunk, scan_rows=scan_rows),
        grid=(N_JT, n_blocks),
        in_specs=[pl.BlockSpec((1, rows_per_block, width), lambda j, b: (j, b, 0)),
                  pl.BlockSpec((1, 1, seqs, 2 * PLANE), lambda j, b: (j, b, 0, 0)),
                  pl.BlockSpec((1,) + w.shape[1:], lambda j, b: (j, 0, 0, 0)),
                  pl.BlockSpec((1,) + pw.shape[1:], lambda j, b: (j, 0, 0)),
                  pl.BlockSpec((1,) + qw.shape[1:], lambda j, b: (j, 0, 0, 0)),
                  pl.BlockSpec((1,) + a_pow.shape[1:], lambda j, b: (j, 0, 0))],
        out_specs=[pl.BlockSpec((1, rows_per_block, width), lambda j, b: (j, b, 0)),
                   pl.BlockSpec((1, 1, seqs, 2 * PLANE), lambda j, b: (j, b, 0, 0))],
        out_shape=[jax.ShapeDtypeStruct(uc.shape, F32),
                   jax.ShapeDtypeStruct(h0.shape, F32)],
        scratch_shapes=[pltpu.VMEM((rows_per_block, 2 * PLANE), F32)] * 2,
        compiler_params=_params(2),
        name=f"ssm_chunk{chunk}",
    )(uc, h0, w, pw, qw, a_pow)


def _ssm_tables(a_re, a_im, log_dt, b_re, b_im, c_re, c_im, chunk):
    hi = lax.Precision.HIGHEST
    dt = jnp.exp(log_dt)[:, None]
    den = a_re * a_re + a_im * a_im
    er, ang = jnp.exp(a_re * dt), a_im * dt
    ab_re, ab_im = er * jnp.cos(ang), er * jnp.sin(ang)
    nr, ni = ab_re - 1.0, ab_im
    f_re, f_im = (nr * a_re + ni * a_im) / den, (ni * a_re - nr * a_im) / den
    bb_re = f_re[..., None] * b_re - f_im[..., None] * b_im
    bb_im = f_re[..., None] * b_im + f_im[..., None] * b_re

    def power(taus):
        taus = jnp.asarray(taus, F32)[:, None, None]
        mag, ang_t = jnp.exp(a_re * dt * taus), a_im * dt * taus
        return mag * jnp.cos(ang_t), mag * jnp.sin(ang_t)

    p_re, p_im = power(np.arange(chunk + 1))
    pr, pi = (x[..., None] for x in power(chunk - 1 - np.arange(chunk)))
    s_re, s_im = pr * bb_re - pi * bb_im, pr * bb_im + pi * bb_re
    qr, qi = p_re[1:, :, None, :], p_im[1:, :, None, :]
    e_re, e_im = c_re * qr - c_im * qi, c_re * qi + c_im * qr
    lr, li = p_re[:chunk, :, None, :], p_im[:chunk, :, None, :]
    cl_re, cl_im = c_re * lr - c_im * li, c_re * li + c_im * lr
    kern = (jnp.einsum("tgon,gni->tgoi", cl_re, bb_re, precision=hi)
            - jnp.einsum("tgon,gni->tgoi", cl_im, bb_im, precision=hi))

    eye = jnp.eye(GROUPS_PER_TILE, dtype=F32)
    n_pair = chunk // TOK_PER_PAIR
    tile = lambda x: x.reshape((x.shape[0], N_JT, GROUPS_PER_TILE) + x.shape[2:])
    kern_t = tile(kern)
    kern_t = jnp.concatenate([kern_t, jnp.zeros_like(kern_t[:1])], axis=0)
    lag = (TOK_PER_PAIR * np.arange(n_pair)[:, None, None]
           + np.arange(TOK_PER_PAIR)[None, None, :] - np.arange(TOK_PER_PAIR)[None, :, None])
    kl = kern_t[lag]
    w = jnp.einsum("dpqjgoi,gh->jdpgiqho", kl, eye)
    w = w.reshape(N_JT, n_pair, MXU, MXU)
    s_t = jnp.stack([tile(s_re), tile(s_im)], axis=0)
    pw = jnp.einsum("ptjgni,gh->jtgiphn", s_t, eye)
    pw = pw.reshape(N_JT, chunk * LANE, 2 * PLANE)
    e_t = jnp.stack([tile(e_re), -tile(e_im)], axis=0)
    qw = jnp.einsum("ptjgon,gh->jpgntho", e_t, eye)
    qw = qw.reshape(N_JT, 2 * PLANE, n_pair, MXU).transpose(0, 2, 1, 3)
    a_pow = jnp.stack([p_re[chunk].reshape(N_JT, PLANE), p_im[chunk].reshape(N_JT, PLANE)], axis=1)
    return w.astype(BF16), pw.astype(BF16), qw.astype(BF16), a_pow


def _state_to_tiles(re, im, n_blocks):
    n_seq = re.shape[0]
    t = lambda x: x.reshape(n_blocks, n_seq // n_blocks, N_JT, PLANE).transpose(2, 0, 1, 3)
    return jnp.concatenate([t(re), t(im)], axis=-1)


def _tiles_to_state(h):
    n_seq = h.shape[1] * h.shape[2]
    t = lambda x: x.transpose(1, 2, 0, 3).reshape(n_seq, N_GROUPS, SSM_STATE)
    return t(h[..., :PLANE]), t(h[..., PLANE:])


def _postmix_kernel(*refs, dils, chunk, final):
    it = iter(refs)
    x1_ref = next(it)
    att_refs = [(next(it), next(it)) for _ in dils] if len(dils) > 1 else [(next(it), None)]
    y_ref, uf_ref, p_ref = next(it), next(it), next(it)
    (natt_ref, nssm_ref, dskip_ref, wglu_ref, bglu_ref, wout_ref, nf_ref, wg_ref, wu_ref, wd_ref,
     nple_ref, wpg_ref, wpp_ref) = (next(it) for _ in range(13))
    perm_t = {d: next(it) for d in sorted((set(dils) | {chunk}) - {1})}
    nfin_ref = next(it) if final else None
    out_ref, ybuf = next(it), next(it)
    tm = x1_ref.shape[0]

    def natural(ref, d):
        val = ref[...] if d == 1 else ref[0].reshape(tm, D_ATT)
        return val if d == 1 else _permute_f32(perm_t[d], val)

    if len(dils) == 1:
        att = att_refs[0][0][...]
    else:
        outs = [natural(o, d) for d, (o, _) in zip(dils, att_refs)]
        lses = [natural(l, d) for d, (_, l) in zip(dils, att_refs)]
        top = functools.reduce(jnp.maximum, lses)
        ws = [jnp.exp(l - top) for l in lses]
        att = sum(w * o for w, o in zip(ws, outs)) / sum(ws)
    a_n = _rms(att, natt_ref[...])
    rows = tm // chunk
    for t in range(chunk):
        for j in range(N_JT):
            ybuf[t * rows:(t + 1) * rows, j * LANE:(j + 1) * LANE] = y_ref[j, :, t * LANE:(t + 1) * LANE]
    y = _permute_f32(perm_t[chunk], ybuf[...]) + dskip_ref[...] * uf_ref[...]
    z = jax.nn.gelu(y)
    s = z * jax.nn.sigmoid(_dot(z.astype(BF16), wglu_ref[...]) + bglu_ref[...])
    s_n = _rms(s, nssm_ref[...])
    mixed = jnp.concatenate([a_n, s_n], axis=-1).astype(BF16)
    x2 = x1_ref[...] + _dot(mixed, wout_ref[...])
    x3 = _ffn(x2, nf_ref[...], wg_ref, wu_ref, wd_ref)
    gate = jax.nn.sigmoid(_dot(_rms(x3, nple_ref[...]).astype(BF16), wpg_ref[...]))
    x4 = x3 + gate * _dot(p_ref[...].astype(BF16), wpp_ref[...])
    out_ref[...] = _rms(x4, nfin_ref[...]) if final else x4


def _postmix(x1, att_planes, dils, y, uf, p, weights, perms_t, nfin, tm, seq, chunk):
    m = x1.shape[0]
    per_seq = seq // tm
    final = nfin is not None
    row = lambda w: pl.BlockSpec((tm, w), lambda i: (i, 0))
    att_args, att_specs = [], []
    for d, (o, l) in zip(dils, att_planes):
        spec = row(D_ATT) if d == 1 else pl.BlockSpec((1, d, tm // d, D_ATT),
                                                       lambda i: (i // per_seq, 0, i % per_seq, 0))
        for a in ((o, l) if len(dils) > 1 else (o,)):
            att_args.append(a.reshape(m, D_ATT) if d == 1 else a)
            att_specs.append(spec)
    consts = list(weights) + [perms_t[d] for d in sorted((set(dils) | {chunk}) - {1})] + ([nfin] if final else [])
    args = [x1] + att_args + [y, uf, p] + consts
    in_specs = ([row(D_MODEL)] + att_specs
                + [pl.BlockSpec((N_JT, tm // chunk, chunk * LANE), lambda i: (0, i, 0)), row(D_SSM), row(D_PLE)]
                + [_const_spec(c.shape) for c in consts])
    return pl.pallas_call(
        functools.partial(_postmix_kernel, dils=tuple(dils), chunk=chunk, final=final),
        grid=(m // tm,),
        in_specs=in_specs,
        out_specs=row(D_MODEL),
        out_shape=jax.ShapeDtypeStruct((m, D_MODEL), F32),
        scratch_shapes=[pltpu.VMEM((tm, D_SSM), F32)],
        compiler_params=_params(1),
        name="postmix",
    )(*args)


def _window_kernel(old_ref, new_ref, out_ref):
    t_new = new_ref.shape[2]
    keep = old_ref.shape[2] - t_new
    out_ref[0, 0, 0:keep] = old_ref[0, 0, t_new:]
    out_ref[0, 0, keep:] = new_ref[0, 0]


def _window_update(old, new):
    depth, batch, length = old.shape[:3]
    t_new = new.shape[2]
    slab = lambda n: pl.BlockSpec((1, 1, n, N_HEADS, HEAD_DIM), lambda l, b: (l, b, 0, 0, 0))
    return pl.pallas_call(
        _window_kernel,
        grid=(depth, batch),
        in_specs=[slab(length), slab(t_new)],
        out_specs=slab(length),
        out_shape=jax.ShapeDtypeStruct(old.shape, old.dtype),
        compiler_params=_params(2),
        name="window_update",
    )(old, new)


def _t5_bucket(dist):
    dist = np.asarray(dist, dtype=np.int64)
    exact = N_BUCKETS // 2
    ratio = np.log(np.maximum(dist, 1) / exact) / np.log(MAX_DISTANCE / exact)
    large = np.minimum(exact + (ratio * (N_BUCKETS - exact)).astype(np.int64), N_BUCKETS - 1)
    return np.where(dist < exact, dist, large).astype(np.int32)


def _branch_bias(rel_bias, dil):
    return rel_bias[_t5_bucket(np.arange(N_STEP + 1) * dil)].T.astype(F32)


def _prompt_table(bias):
    step = np.arange(N_STEP)[:, None] - np.arange(2 * N_STEP)[None, :] + N_STEP
    valid = (step >= 0) & (step <= N_STEP)
    return jnp.where(valid[None], bias[:, np.clip(step, 0, N_STEP)], NEG)


def _sample_table(biases, lay, t_new):
    total = lay["total"]
    branch_of = np.zeros((Q_PAD, total), np.int32) - 1
    step_of = np.zeros((Q_PAD, total), np.int32)
    branch_of[:t_new], step_of[:t_new] = lay["branch_of"], lay["step_of"]
    stacked = jnp.stack(biases, axis=0)
    vals = stacked[np.maximum(branch_of, 0), :, step_of]
    tab = jnp.where((branch_of >= 0)[..., None], vals, NEG)
    tab = tab.at[t_new:].set(0.0)
    return tab.transpose(2, 0, 1)


def kernel(x_prompt, x_sample, p_prompt, p_sample, cache_k, cache_v, state_ssm_re, state_ssm_im,
           rel_bias, w_in, w_out, norm_mix, norm_att_out, norm_ssm_out, norm_ffn, ffn_w_gate,
           ffn_w_up, ffn_w_down, ssm_a_re, ssm_a_im, ssm_log_dt, ssm_b_re, ssm_b_im, ssm_c_re,
           ssm_c_im, ssm_d, w_glu, b_glu, norm_ple, w_ple_gate, w_ple_proj, norm_final):
    depth = w_in.shape[0]
    batch, seq, _ = x_prompt.shape
    dec_batch, dec_seq, _ = x_sample.shape
    cache_len = cache_k.shape[2]
    keep = min(BRANCHES[-1][0], seq)
    dils = tuple(d for _, d in BRANCHES)
    chunk_p = dils[-1]
    tm_p = 256
    tm_s = dec_batch * dec_seq
    assert seq % tm_p == 0 and chunk_p in dils and dec_seq % TOK_PER_PAIR == 0 and tm_s % 8 == 0

    xp = x_prompt.reshape(batch * seq, D_MODEL)
    xs = x_sample.reshape(tm_s, D_MODEL)
    lay = _sample_layout(cache_len, dec_seq)
    biases = [_branch_bias(rel_bias, d) for d in dils]
    tabs_p = [_prompt_table(b) for b in biases]
    tab_s = _sample_table(biases, lay, dec_seq)
    perms_p = {d: jnp.asarray(_perm_matrix(tm_p, d), BF16) for d in dils if d > 1}
    perms_pt = {d: jnp.asarray(_perm_matrix(tm_p, d).T, BF16) for d in dils if d > 1}
    perms_s = {dec_seq: jnp.asarray(_perm_matrix(tm_s, dec_seq), BF16)}
    perms_st = {dec_seq: jnp.asarray(_perm_matrix(tm_s, dec_seq).T, BF16)}
    row = lambda v: v.reshape(1, -1)
    bf = lambda a: a.astype(BF16)

    def layer_weights(i):
        pre = (row(norm_ffn[i, 0]), bf(ffn_w_gate[i, 0]), bf(ffn_w_up[i, 0]), bf(ffn_w_down[i, 0]),
               row(norm_mix[i]), bf(w_in[i]))
        post = (row(norm_att_out[i]), row(norm_ssm_out[i]), row(ssm_d[i]), bf(w_glu[i]), row(b_glu[i]),
                bf(w_out[i]), row(norm_ffn[i, 1]), bf(ffn_w_gate[i, 1]), bf(ffn_w_up[i, 1]),
                bf(ffn_w_down[i, 1]), row(norm_ple[i]), bf(w_ple_gate[i]), bf(w_ple_proj[i]))
        nfin = row(norm_final) if i == depth - 1 else None
        ssm_par = (ssm_a_re[i], ssm_a_im[i], ssm_log_dt[i], ssm_b_re[i], ssm_b_im[i], ssm_c_re[i], ssm_c_im[i])
        return pre, post, nfin, ssm_par

    k_news, v_news, rss, iss = [], [], [], []
    for i in range(depth):
        pre, post, nfin, ssm_par = layer_weights(i)
        x1, q, k, v, kf, vf, uf, uc = _premix(xs, pre, perms_s, tm_s, tm_s, tm_s, (), dec_seq)
        att = _attn_sample(q, k, v, cache_k, cache_v, i, tab_s, lay, dec_batch, dec_seq)
        h0 = _state_to_tiles(state_ssm_re[i], state_ssm_im[i], 1)
        y, h_last = _ssm(uc, h0, _ssm_tables(*ssm_par, dec_seq), dec_seq, dec_batch, False)
        xs = _postmix(x1, [(att, None)], (1,), y, uf, p_sample[i].reshape(tm_s, D_PLE), post, perms_st, nfin,
                      tm_s, tm_s, dec_seq)
        k_news.append(kf.reshape(dec_batch, dec_seq, N_HEADS, HEAD_DIM))
        v_news.append(vf.reshape(dec_batch, dec_seq, N_HEADS, HEAD_DIM))
        re, im = _tiles_to_state(h_last)
        rss.append(re)
        iss.append(im)

    kp, vp, rp, ip = [], [], [], []
    for i in range(depth):
        pre, post, nfin, ssm_par = layer_weights(i)
        outs = _premix(xp, pre, perms_p, tm_p, seq, keep, dils[1:], chunk_p)
        x1, q, k, v = outs[:4]
        kf, vf, uf, uc = outs[-4:]
        nat = lambda a: a.reshape(batch, 1, seq, D_ATT)
        qkv = [(nat(q), nat(k), nat(v))] + [tuple(outs[4 + 3 * n:7 + 3 * n]) for n in range(len(dils) - 1)]
        planes = [_attn_prompt_branch(*qkv[bi], tabs_p[bi]) for bi in range(len(dils))]
        zeros = jnp.zeros((batch, N_GROUPS, SSM_STATE), F32)
        y, h_last = _ssm(uc, _state_to_tiles(zeros, zeros, batch), _ssm_tables(*ssm_par, chunk_p),
                         chunk_p, seq // chunk_p, True)
        xp = _postmix(x1, planes, dils, y, uf, p_prompt[i].reshape(batch * seq, D_PLE), post, perms_pt,
                      nfin, tm_p, seq, chunk_p)
        kp.append(kf)
        vp.append(vf)
        re, im = _tiles_to_state(h_last)
        rp.append(re)
        ip.append(im)

    return (xp.reshape(batch, seq, D_MODEL), xs.reshape(dec_batch, dec_seq, D_MODEL),
            jnp.stack(kp), jnp.stack(vp), jnp.stack(rp), jnp.stack(ip),
            _window_update(cache_k, jnp.stack(k_news)), _window_update(cache_v, jnp.stack(v_news)),
            jnp.stack(rss), jnp.stack(iss))
```

```python
import functools
import math

import numpy as np
import jax
import jax.numpy as jnp
from jax import lax
from jax.experimental import pallas as pl
from jax.experimental.pallas import tpu as pltpu

F32 = jnp.float32
BF16 = jnp.bfloat16

D_MODEL = 1024
D_ATT = 512
D_SSM = 512
HEAD_DIM = 64
N_HEADS = 8
SSM_GROUP = 16
N_GROUPS = 32
SSM_STATE = 64
BRANCHES = ((128, 1), (512, 4), (2048, 16))
N_STEP = 128
N_BUCKETS = 32
MAX_DISTANCE = 2048
D_FF = 2816
D_PLE = 256
D_IN = 3 * D_ATT + D_SSM
EPS = 1e-6
NEG = -1e30

LANE = 128
MXU = 256
GROUPS_PER_TILE = LANE // SSM_GROUP
N_JT = D_SSM // LANE
PLANE = GROUPS_PER_TILE * SSM_STATE
TOK_PER_PAIR = MXU // LANE
FF_CHUNK = 256
VMEM_LIMIT = 56 * 1024 * 1024

assert all(w // d == N_STEP for w, d in BRANCHES)
assert D_FF % FF_CHUNK == 0


def _rms(x, g):
    return x * lax.rsqrt(jnp.mean(x * x, axis=-1, keepdims=True) + EPS) * g


def _dot(a, b):
    return jnp.dot(a, b, preferred_element_type=F32)


def _ffn(x, g, wg_ref, wu_ref, wd_ref):
    h = _rms(x, g).astype(BF16)
    acc = jnp.zeros(x.shape, F32)
    for c in range(D_FF // FF_CHUNK):
        sl = slice(c * FF_CHUNK, (c + 1) * FF_CHUNK)
        gate = _dot(h, wg_ref[:, sl])
        up = _dot(h, wu_ref[:, sl])
        act = (jax.nn.silu(gate) * up).astype(BF16)
        acc = acc + _dot(act, wd_ref[sl, :])
    return x + 0.5 * acc


def _perm_matrix(n, d):
    p = np.zeros((n, n), np.float32)
    s, r = np.meshgrid(np.arange(n // d), np.arange(d), indexing="ij")
    p[(r * (n // d) + s).ravel(), (d * s + r).ravel()] = 1.0
    return p


def _permute_f32(pt_ref, x):
    hi = x.astype(BF16)
    rest = x - hi.astype(F32)
    mid = rest.astype(BF16)
    lo = (rest - mid.astype(F32)).astype(BF16)
    pt = pt_ref[...]
    return _dot(pt, hi) + _dot(pt, mid) + _dot(pt, lo)


def _const_spec(shape):
    nd = len(shape)
    return pl.BlockSpec(shape, lambda *_: (0,) * nd, pipeline_mode=pl.Buffered(1))


def _params(n_axes):
    return pltpu.CompilerParams(dimension_semantics=("arbitrary",) * n_axes,
                                vmem_limit_bytes=VMEM_LIMIT)


def _premix_kernel(*refs, dils, chunk, per_seq, skip):
    it = iter(refs)
    x_ref, nf_ref, wg_ref, wu_ref, wd_ref, nm_ref, win_ref = (next(it) for _ in range(7))
    perm = {d: next(it) for d in sorted(set(dils) | {chunk})}
    x1_ref, q_ref, k_ref, v_ref = (next(it) for _ in range(4))
    strided = {d: (next(it), next(it), next(it)) for d in dils}
    kf_ref, vf_ref, uf_ref, uc_ref = (next(it) for _ in range(4))

    x1 = _ffn(x_ref[...], nf_ref[...], wg_ref, wu_ref, wd_ref)
    x1_ref[...] = x1
    h = _rms(x1, nm_ref[...]).astype(BF16)
    qkvu = _dot(h, win_ref[...])
    q = (qkvu[:, :D_ATT] * (1.0 / math.sqrt(HEAD_DIM))).astype(BF16)
    kf = qkvu[:, D_ATT:2 * D_ATT]
    vf = qkvu[:, 2 * D_ATT:3 * D_ATT]
    uf = qkvu[:, 3 * D_ATT:]
    k, v, u = kf.astype(BF16), vf.astype(BF16), uf.astype(BF16)
    q_ref[...], k_ref[...], v_ref[...] = q, k, v
    uf_ref[...] = uf
    tm = x1.shape[0]

    @pl.when(pl.program_id(0) % per_seq >= skip)
    def _():
        kf_ref[0] = kf.T.reshape(N_HEADS, HEAD_DIM, tm)
        vf_ref[0] = vf.T.reshape(N_HEADS, HEAD_DIM, tm)

    for d, outs in strided.items():
        for val, out in zip((q, k, v), outs):
            out[0] = _dot(perm[d][...], val).astype(BF16).reshape(d, tm // d, D_ATT)
    up = _dot(perm[chunk][...], u).astype(BF16)
    rows = tm // chunk
    for t in range(chunk):
        for j in range(N_JT):
            uc_ref[j, :, t * LANE:(t + 1) * LANE] = up[t * rows:(t + 1) * rows, j * LANE:(j + 1) * LANE]


def _premix(x, weights, perms, tm, seq, keep, dils, chunk):
    m = x.shape[0]
    per_seq = seq // tm
    skip = (seq - keep) // tm
    assert keep % tm == 0 and tm % LANE == 0
    kept = pl.BlockSpec((1, N_HEADS, HEAD_DIM, tm),
                        lambda i: (i // per_seq, 0, 0, jnp.maximum(i % per_seq - skip, 0)))
    kept_shape = jax.ShapeDtypeStruct((m // seq, N_HEADS, HEAD_DIM, keep), F32)
    row = lambda w: pl.BlockSpec((tm, w), lambda i: (i, 0))
    consts = list(weights) + [perms[d] for d in sorted(set(dils) | {chunk})]
    in_specs = [row(D_MODEL)] + [_const_spec(c.shape) for c in consts]
    out_specs = [row(D_MODEL), row(D_ATT), row(D_ATT), row(D_ATT)]
    out_shape = [jax.ShapeDtypeStruct((m, D_MODEL), F32)] + [jax.ShapeDtypeStruct((m, D_ATT), BF16)] * 3
    for d in dils:
        out_specs += [pl.BlockSpec((1, d, tm // d, D_ATT), lambda i: (i // per_seq, 0, i % per_seq, 0))] * 3
        out_shape += [jax.ShapeDtypeStruct((m // seq, d, seq // d, D_ATT), BF16)] * 3
    out_specs += [kept, kept, row(D_SSM), pl.BlockSpec((N_JT, tm // chunk, chunk * LANE), lambda i: (0, i, 0))]
    out_shape += [kept_shape, kept_shape, jax.ShapeDtypeStruct((m, D_SSM), F32),
                  jax.ShapeDtypeStruct((N_JT, m // chunk, chunk * LANE), BF16)]
    return pl.pallas_call(
        functools.partial(_premix_kernel, dils=tuple(dils), chunk=chunk, per_seq=per_seq, skip=skip),
        grid=(m // tm,),
        in_specs=in_specs, out_specs=out_specs, out_shape=out_shape,
        compiler_params=_params(1),
        name="premix",
    )(x, *consts)


def _attn_prompt_kernel(q_ref, kc_ref, kp_ref, vc_ref, vp_ref, tab_ref, o_ref, l_ref, kbuf, vbuf, *, qb):
    first = pl.program_id(2) == 0
    kbuf[0:N_STEP, :] = kp_ref[0, 0]
    kbuf[N_STEP:, :] = kc_ref[0, 0]
    vbuf[0:N_STEP, :] = vp_ref[0, 0]
    vbuf[N_STEP:, :] = vc_ref[0, 0]
    nk = 2 * N_STEP
    lo_half = lax.broadcasted_iota(jnp.int32, (N_STEP, LANE), 1) < HEAD_DIM
    keep_lo = jnp.where(lo_half, 1.0, 0.0).astype(BF16)
    keep_hi = jnp.where(lo_half, 0.0, 1.0).astype(BF16)
    col = lax.broadcasted_iota(jnp.int32, (N_STEP, nk), 1)
    before_start = jnp.where((col < N_STEP) & first, NEG, 0.0)

    def block(qi, carry):
        r0 = pl.multiple_of(qi * N_STEP, N_STEP)
        edge = jnp.where(qi == 0, before_start, 0.0)
        for hp in range(N_HEADS // 2):
            lanes = slice(hp * LANE, (hp + 1) * LANE)
            qp = q_ref[0, 0, pl.ds(r0, N_STEP), lanes]
            kp = kbuf[pl.ds(r0, nk), lanes]
            vp = vbuf[pl.ds(r0, nk), lanes]
            o_pair, l_pair = None, None
            for hh in range(2):
                qm = qp * (keep_lo if hh == 0 else keep_hi)
                s = lax.dot_general(qm, kp, (((1,), (1,)), ((), ())), preferred_element_type=F32)
                s = s + tab_ref[2 * hp + hh] + edge
                m = jnp.max(s, axis=-1, keepdims=True)
                e = jnp.exp(s - m)
                den = jnp.sum(e, axis=-1, keepdims=True)
                o = _dot(e.astype(BF16), vp) / den
                lse = jnp.broadcast_to(m + jnp.log(den), (N_STEP, LANE))
                if hh == 0:
                    o_pair, l_pair = o, lse
                else:
                    o_pair = jnp.where(lo_half, o_pair, o)
                    l_pair = jnp.where(lo_half, l_pair, lse)
            o_ref[0, 0, pl.ds(r0, N_STEP), lanes] = o_pair
            l_ref[0, 0, pl.ds(r0, N_STEP), lanes] = l_pair
        return carry

    lax.fori_loop(0, qb // N_STEP, block, 0)


def _attn_prompt_branch(q, k, v, tab):
    batch, dil, sub, _ = q.shape
    qb = min(sub, 512)
    assert sub % qb == 0 and qb % N_STEP == 0
    per = qb // N_STEP
    cur = pl.BlockSpec((1, 1, qb, D_ATT), lambda b, r, i: (b, r, i, 0))
    prev = pl.BlockSpec((1, 1, N_STEP, D_ATT), lambda b, r, i: (b, r, jnp.maximum(i * per - 1, 0), 0))
    return pl.pallas_call(
        functools.partial(_attn_prompt_kernel, qb=qb),
        grid=(batch, dil, sub // qb),
        in_specs=[cur, cur, prev, cur, prev, _const_spec(tab.shape)],
        out_specs=[cur, cur],
        out_shape=[jax.ShapeDtypeStruct(q.shape, F32)] * 2,
        scratch_shapes=[pltpu.VMEM((qb + N_STEP, D_ATT), BF16)] * 2,
        compiler_params=_params(3),
        name=f"attn_prompt_d{dil}",
    )(q, k, k, v, v, tab)


Q_PAD = 8


def _sample_steps(cache_len, t_new, n_new):
    step_of = np.full((len(BRANCHES), t_new, cache_len + n_new), -1, np.int32)
    for bi, (w, d) in enumerate(BRANCHES):
        for i in range(t_new):
            for j in range(w // d + 1):
                ext = cache_len + i - j * d
                assert ext >= 0
                step_of[bi, i, ext] = j
    return step_of


def _attn_sample_kernel(q_ref, kn_ref, vn_ref, kc_ref, vc_ref, tab_ref, o_ref, qs, *, t_new):
    n_new = kn_ref.shape[2]
    to_front = (n_new - pl.program_id(0) * t_new) % n_new
    q = q_ref[0].astype(F32)
    qs[...] = jnp.zeros(qs.shape, F32)
    for h in range(N_HEADS):
        qs[h, 0:t_new, :] = q[:, h * HEAD_DIM:(h + 1) * HEAD_DIM]
        qh = qs[h].astype(BF16)
        kn = pltpu.roll(kn_ref[h], to_front, 1).astype(BF16)
        vn = pltpu.roll(vn_ref[h], to_front, 1).astype(BF16)
        s = jnp.concatenate([_dot(qh, kc_ref[0, h].astype(BF16)), _dot(qh, kn)], axis=-1)
        logits = [s + tab_ref[bi, h] for bi in range(len(BRANCHES))]
        m = jnp.max(functools.reduce(jnp.maximum, logits), axis=-1, keepdims=True)
        e = sum(jnp.exp(l - m) for l in logits)
        den = jnp.sum(e, axis=-1, keepdims=True)
        eb = e.astype(BF16)
        nt = (((1,), (1,)), ((), ()))
        o = (lax.dot_general(eb[:, :-n_new], vc_ref[0, h].astype(BF16), nt, preferred_element_type=F32)
             + lax.dot_general(eb[:, -n_new:], vn, nt, preferred_element_type=F32)) / den
        o_ref[0, :, h * HEAD_DIM:(h + 1) * HEAD_DIM] = o[:t_new]


def _attn_sample(q, k_new, v_new, cache_k, cache_v, layer, tab, batch, t_new):
    cache_len = cache_k.shape[-1]
    tok = pl.BlockSpec((1, t_new, D_ATT), lambda b: (b, 0, 0))
    slab = pl.BlockSpec((pl.Squeezed(), 1, N_HEADS, HEAD_DIM, cache_len), lambda b: (layer, b, 0, 0, 0))
    out = pl.pallas_call(
        functools.partial(_attn_sample_kernel, t_new=t_new),
        grid=(batch,),
        in_specs=[tok, _const_spec(k_new.shape), _const_spec(v_new.shape), slab, slab, _const_spec(tab.shape)],
        out_specs=tok,
        out_shape=jax.ShapeDtypeStruct((batch, t_new, D_ATT), F32),
        scratch_shapes=[pltpu.VMEM((N_HEADS, Q_PAD, HEAD_DIM), F32)],
        compiler_params=_params(1),
        name="attn_sample",
    )(q.reshape(batch, t_new, D_ATT), k_new, v_new, cache_k, cache_v, tab)
    return out.reshape(batch * t_new, D_ATT)


def _ssm_kernel(u_ref, h0_ref, w_ref, pw_ref, qw_ref, a_ref, y_ref, hl_ref, z_buf, hb_buf,
                *, chunk, scan_rows):
    n_pair = chunk // TOK_PER_PAIR
    ub = u_ref[0]
    z = _dot(ub, pw_ref[0])
    ar = a_ref[0, 0:1, :]
    ai = a_ref[0, 1:2, :]
    h0 = h0_ref[0, 0]
    if scan_rows:
        z_buf[...] = z

        def step(r, h):
            hr, hi = h
            hb_buf[pl.ds(r, 1), 0:PLANE] = hr
            hb_buf[pl.ds(r, 1), PLANE:] = hi
            zr = z_buf[pl.ds(r, 1), 0:PLANE]
            zi = z_buf[pl.ds(r, 1), PLANE:]
            return ar * hr - ai * hi + zr, ar * hi + ai * hr + zi

        hr, hi = lax.fori_loop(0, ub.shape[0], step, (h0[:, 0:PLANE], h0[:, PLANE:]))
        hl_ref[0, 0, :, 0:PLANE] = hr
        hl_ref[0, 0, :, PLANE:] = hi
        hb = hb_buf[...].astype(BF16)
    else:
        hr, hi = h0[:, 0:PLANE], h0[:, PLANE:]
        hl_ref[0, 0, :, 0:PLANE] = ar * hr - ai * hi + z[:, 0:PLANE]
        hl_ref[0, 0, :, PLANE:] = ar * hi + ai * hr + z[:, PLANE:]
        hb = h0.astype(BF16)
    for b in range(n_pair):
        acc = _dot(hb, qw_ref[0, b])
        for a in range(b + 1):
            acc = acc + _dot(ub[:, a * MXU:(a + 1) * MXU], w_ref[0, b - a])
        y_ref[0, :, b * MXU:(b + 1) * MXU] = acc


def _ssm(uc, h0, tabs, chunk, rows_per_block, scan_rows):
    w, pw, qw, a_pow = tabs
    rows, width = uc.shape[1:]
    n_blocks = rows // rows_per_block
    seqs = h0.shape[2]
    return pl.pallas_call(
        functools.partial(_ssm_kernel, chunk=chunk, scan_rows=scan_rows),
        grid=(N_JT, n_blocks),
        in_specs=[pl.BlockSpec((1, rows_per_block, width), lambda j, b: (j, b, 0)),
                  pl.BlockSpec((1, 1, seqs, 2 * PLANE), lambda j, b: (j, b, 0, 0)),
                  pl.BlockSpec((1,) + w.shape[1:], lambda j, b: (j, 0, 0, 0)),
                  pl.BlockSpec((1,) + pw.shape[1:], lambda j, b: (j, 0, 0)),
                  pl.BlockSpec((1,) + qw.shape[1:], lambda j, b: (j, 0, 0, 0)),
                  pl.BlockSpec((1,) + a_pow.shape[1:], lambda j, b: (j, 0, 0))],
        out_specs=[pl.BlockSpec((1, rows_per_block, width), lambda j, b: (j, b, 0)),
                   pl.BlockSpec((1, 1, seqs, 2 * PLANE), lambda j, b: (j, b, 0, 0))],
        out_shape=[jax.ShapeDtypeStruct(uc.shape, F32),
                   jax.ShapeDtypeStruct(h0.shape, F32)],
        scratch_shapes=[pltpu.VMEM((rows_per_block, 2 * PLANE), F32)] * 2,
        compiler_params=_params(2),
        name=f"ssm_chunk{chunk}",
    )(uc, h0, w, pw, qw, a_pow)


def _ssm_tables(a_re, a_im, log_dt, b_re, b_im, c_re, c_im, chunk):
    hi = lax.Precision.HIGHEST
    dt = jnp.exp(log_dt)[:, None]
    den = a_re * a_re + a_im * a_im
    er, ang = jnp.exp(a_re * dt), a_im * dt
    ab_re, ab_im = er * jnp.cos(ang), er * jnp.sin(ang)
    nr, ni = ab_re - 1.0, ab_im
    f_re, f_im = (nr * a_re + ni * a_im) / den, (ni * a_re - nr * a_im) / den
    bb_re = f_re[..., None] * b_re - f_im[..., None] * b_im
    bb_im = f_re[..., None] * b_im + f_im[..., None] * b_re

    def power(taus):
        taus = jnp.asarray(taus, F32)[:, None, None]
        mag, ang_t = jnp.exp(a_re * dt * taus), a_im * dt * taus
        return mag * jnp.cos(ang_t), mag * jnp.sin(ang_t)

    p_re, p_im = power(np.arange(chunk + 1))
    pr, pi = (x[..., None] for x in power(chunk - 1 - np.arange(chunk)))
    s_re, s_im = pr * bb_re - pi * bb_im, pr * bb_im + pi * bb_re
    qr, qi = p_re[1:, :, None, :], p_im[1:, :, None, :]
    e_re, e_im = c_re * qr - c_im * qi, c_re * qi + c_im * qr
    lr, li = p_re[:chunk, :, None, :], p_im[:chunk, :, None, :]
    cl_re, cl_im = c_re * lr - c_im * li, c_re * li + c_im * lr
    kern = (jnp.einsum("tgon,gni->tgoi", cl_re, bb_re, precision=hi)
            - jnp.einsum("tgon,gni->tgoi", cl_im, bb_im, precision=hi))

    eye = jnp.eye(GROUPS_PER_TILE, dtype=F32)
    n_pair = chunk // TOK_PER_PAIR
    tile = lambda x: x.reshape((x.shape[0], N_JT, GROUPS_PER_TILE) + x.shape[2:])
    kern_t = tile(kern)
    kern_t = jnp.concatenate([kern_t, jnp.zeros_like(kern_t[:1])], axis=0)
    lag = (TOK_PER_PAIR * np.arange(n_pair)[:, None, None]
           + np.arange(TOK_PER_PAIR)[None, None, :] - np.arange(TOK_PER_PAIR)[None, :, None])
    kl = kern_t[lag]
    w = jnp.einsum("dpqjgoi,gh->jdpgiqho", kl, eye)
    w = w.reshape(N_JT, n_pair, MXU, MXU)
    s_t = jnp.stack([tile(s_re), tile(s_im)], axis=0)
    pw = jnp.einsum("ptjgni,gh->jtgiphn", s_t, eye)
    pw = pw.reshape(N_JT, chunk * LANE, 2 * PLANE)
    e_t = jnp.stack([tile(e_re), -tile(e_im)], axis=0)
    qw = jnp.einsum("ptjgon,gh->jpgntho", e_t, eye)
    qw = qw.reshape(N_JT, 2 * PLANE, n_pair, MXU).transpose(0, 2, 1, 3)
    a_pow = jnp.stack([p_re[chunk].reshape(N_JT, PLANE), p_im[chunk].reshape(N_JT, PLANE)], axis=1)
    return w.astype(BF16), pw.astype(BF16), qw.astype(BF16), a_pow


def _state_to_tiles(re, im, n_blocks):
    n_seq = re.shape[0]
    t = lambda x: x.reshape(n_blocks, n_seq // n_blocks, N_JT, PLANE).transpose(2, 0, 1, 3)
    return jnp.concatenate([t(re), t(im)], axis=-1)


def _tiles_to_state(h):
    n_seq = h.shape[1] * h.shape[2]
    t = lambda x: x.transpose(1, 2, 0, 3).reshape(n_seq, N_GROUPS, SSM_STATE)
    return t(h[..., :PLANE]), t(h[..., PLANE:])


def _postmix_kernel(*refs, dils, chunk, final):
    it = iter(refs)
    x1_ref = next(it)
    att_refs = [(next(it), next(it)) for _ in dils] if len(dils) > 1 else [(next(it), None)]
    y_ref, uf_ref, p_ref = next(it), next(it), next(it)
    (natt_ref, nssm_ref, dskip_ref, wglu_ref, bglu_ref, wout_ref, nf_ref, wg_ref, wu_ref, wd_ref,
     nple_ref, wpg_ref, wpp_ref) = (next(it) for _ in range(13))
    perm_t = {d: next(it) for d in sorted((set(dils) | {chunk}) - {1})}
    nfin_ref = next(it) if final else None
    out_ref, ybuf = next(it), next(it)
    tm = x1_ref.shape[0]

    def natural(ref, d):
        val = ref[...] if d == 1 else ref[0].reshape(tm, D_ATT)
        return val if d == 1 else _permute_f32(perm_t[d], val)

    if len(dils) == 1:
        att = att_refs[0][0][...]
    else:
        outs = [natural(o, d) for d, (o, _) in zip(dils, att_refs)]
        lses = [natural(l, d) for d, (_, l) in zip(dils, att_refs)]
        top = functools.reduce(jnp.maximum, lses)
        ws = [jnp.exp(l - top) for l in lses]
        att = sum(w * o for w, o in zip(ws, outs)) / sum(ws)
    a_n = _rms(att, natt_ref[...])
    rows = tm // chunk
    for t in range(chunk):
        for j in range(N_JT):
            ybuf[t * rows:(t + 1) * rows, j * LANE:(j + 1) * LANE] = y_ref[j, :, t * LANE:(t + 1) * LANE]
    y = _permute_f32(perm_t[chunk], ybuf[...]) + dskip_ref[...] * uf_ref[...]
    z = jax.nn.gelu(y)
    s = z * jax.nn.sigmoid(_dot(z.astype(BF16), wglu_ref[...]) + bglu_ref[...])
    s_n = _rms(s, nssm_ref[...])
    mixed = jnp.concatenate([a_n, s_n], axis=-1).astype(BF16)
    x2 = x1_ref[...] + _dot(mixed, wout_ref[...])
    x3 = _ffn(x2, nf_ref[...], wg_ref, wu_ref, wd_ref)
    gate = jax.nn.sigmoid(_dot(_rms(x3, nple_ref[...]).astype(BF16), wpg_ref[...]))
    x4 = x3 + gate * _dot(p_ref[...].astype(BF16), wpp_ref[...])
    out_ref[...] = _rms(x4, nfin_ref[...]) if final else x4


def _postmix(x1, att_planes, dils, y, uf, p, layer, weights, perms_t, nfin, tm, seq, chunk):
    m = x1.shape[0]
    per_seq = seq // tm
    final = nfin is not None
    row = lambda w: pl.BlockSpec((tm, w), lambda i: (i, 0))
    att_args, att_specs = [], []
    for d, (o, l) in zip(dils, att_planes):
        spec = row(D_ATT) if d == 1 else pl.BlockSpec((1, d, tm // d, D_ATT),
                                                       lambda i: (i // per_seq, 0, i % per_seq, 0))
        for a in ((o, l) if len(dils) > 1 else (o,)):
            att_args.append(a.reshape(m, D_ATT) if d == 1 else a)
            att_specs.append(spec)
    consts = list(weights) + [perms_t[d] for d in sorted((set(dils) | {chunk}) - {1})] + ([nfin] if final else [])
    args = [x1] + att_args + [y, uf, p] + consts
    in_specs = ([row(D_MODEL)] + att_specs
                + [pl.BlockSpec((N_JT, tm // chunk, chunk * LANE), lambda i: (0, i, 0)), row(D_SSM),
                   pl.BlockSpec((pl.Squeezed(), tm, D_PLE), lambda i: (layer, i, 0))]
                + [_const_spec(c.shape) for c in consts])
    return pl.pallas_call(
        functools.partial(_postmix_kernel, dils=tuple(dils), chunk=chunk, final=final),
        grid=(m // tm,),
        in_specs=in_specs,
        out_specs=row(D_MODEL),
        out_shape=jax.ShapeDtypeStruct((m, D_MODEL), F32),
        scratch_shapes=[pltpu.VMEM((tm, D_SSM), F32)],
        compiler_params=_params(1),
        name="postmix",
    )(*args)


def _window_kernel(old_ref, new_ref, out_ref, *, t_new):
    length, n_new = old_ref.shape[-1], new_ref.shape[-1]
    to_back = (LANE - t_new - pl.program_id(1) * t_new) % n_new
    is_new = lax.broadcasted_iota(jnp.int32, (HEAD_DIM, LANE), 1) >= LANE - t_new
    for h in range(N_HEADS):
        shifted = pltpu.roll(old_ref[0, 0, h], length - t_new, 1)
        new = pltpu.roll(new_ref[0, h], to_back, 1)[:, :LANE]
        out_ref[0, 0, h, :, :length - LANE] = shifted[:, :length - LANE]
        out_ref[0, 0, h, :, length - LANE:] = jnp.where(is_new, new, shifted[:, length - LANE:])


def _window_update(old, new, t_new):
    depth, batch = old.shape[:2]
    assert new.shape[-1] == batch * t_new and new.shape[-1] % LANE == 0 and t_new <= LANE
    slab = pl.BlockSpec((1, 1) + old.shape[2:], lambda l, b: (l, b, 0, 0, 0))
    return pl.pallas_call(
        functools.partial(_window_kernel, t_new=t_new),
        grid=(depth, batch),
        in_specs=[slab, pl.BlockSpec((1,) + new.shape[1:], lambda l, b: (l, 0, 0, 0))],
        out_specs=slab,
        out_shape=jax.ShapeDtypeStruct(old.shape, old.dtype),
        compiler_params=_params(2),
        name="window_update",
    )(old, new)


def _t5_bucket(dist):
    dist = np.asarray(dist, dtype=np.int64)
    exact = N_BUCKETS // 2
    ratio = np.log(np.maximum(dist, 1) / exact) / np.log(MAX_DISTANCE / exact)
    large = np.minimum(exact + (ratio * (N_BUCKETS - exact)).astype(np.int64), N_BUCKETS - 1)
    return np.where(dist < exact, dist, large).astype(np.int32)


def _branch_bias(rel_bias, dil):
    return rel_bias[_t5_bucket(np.arange(N_STEP + 1) * dil)].T.astype(F32)


def _prompt_table(bias):
    step = np.arange(N_STEP)[:, None] - np.arange(2 * N_STEP)[None, :] + N_STEP
    valid = (step >= 0) & (step <= N_STEP)
    return jnp.where(valid[None], bias[:, np.clip(step, 0, N_STEP)], NEG)


def _sample_table(biases, step_of):
    n_br, t_new, cols = step_of.shape
    tabs = []
    for bi in range(n_br):
        vals = biases[bi][:, np.maximum(step_of[bi], 0)]
        tab = jnp.where((step_of[bi] >= 0)[None], vals, NEG)
        tabs.append(jnp.concatenate([tab, jnp.zeros((N_HEADS, Q_PAD - t_new, cols), F32)], axis=1))
    return jnp.stack(tabs)


def kernel(x_prompt, x_sample, p_prompt, p_sample, cache_k, cache_v, state_ssm_re, state_ssm_im,
           rel_bias, w_in, w_out, norm_mix, norm_att_out, norm_ssm_out, norm_ffn, ffn_w_gate,
           ffn_w_up, ffn_w_down, ssm_a_re, ssm_a_im, ssm_log_dt, ssm_b_re, ssm_b_im, ssm_c_re,
           ssm_c_im, ssm_d, w_glu, b_glu, norm_ple, w_ple_gate, w_ple_proj, norm_final):
    depth = w_in.shape[0]
    batch, seq, _ = x_prompt.shape
    dec_batch, dec_seq, _ = x_sample.shape
    cache_len = cache_k.shape[2]
    keep = min(BRANCHES[-1][0], seq)
    dils = tuple(d for _, d in BRANCHES)
    chunk_p = dils[-1]
    tm_p = 256
    tm_s = dec_batch * dec_seq
    assert seq % tm_p == 0 and chunk_p in dils and dec_seq % TOK_PER_PAIR == 0 and tm_s % 8 == 0

    xp = x_prompt.reshape(batch * seq, D_MODEL)
    xs = x_sample.reshape(tm_s, D_MODEL)
    biases = [_branch_bias(rel_bias, d) for d in dils]
    tabs_p = [_prompt_table(b) for b in biases]
    tab_s = _sample_table(biases, _sample_steps(cache_len, dec_seq, tm_s))
    to_lanes = lambda a: a.transpose(0, 1, 3, 4, 2)
    from_lanes = lambda a: a.transpose(0, 1, 4, 2, 3)
    cache_kt, cache_vt = to_lanes(cache_k), to_lanes(cache_v)
    perms_p = {d: jnp.asarray(_perm_matrix(tm_p, d), BF16) for d in dils if d > 1}
    perms_pt = {d: jnp.asarray(_perm_matrix(tm_p, d).T, BF16) for d in dils if d > 1}
    perms_s = {dec_seq: jnp.asarray(_perm_matrix(tm_s, dec_seq), BF16)}
    perms_st = {dec_seq: jnp.asarray(_perm_matrix(tm_s, dec_seq).T, BF16)}
    row = lambda v: v.reshape(1, -1)
    bf = lambda a: a.astype(BF16)

    def layer_weights(i):
        pre = (row(norm_ffn[i, 0]), bf(ffn_w_gate[i, 0]), bf(ffn_w_up[i, 0]), bf(ffn_w_down[i, 0]),
               row(norm_mix[i]), bf(w_in[i]))
        post = (row(norm_att_out[i]), row(norm_ssm_out[i]), row(ssm_d[i]), bf(w_glu[i]), row(b_glu[i]),
                bf(w_out[i]), row(norm_ffn[i, 1]), bf(ffn_w_gate[i, 1]), bf(ffn_w_up[i, 1]),
                bf(ffn_w_down[i, 1]), row(norm_ple[i]), bf(w_ple_gate[i]), bf(w_ple_proj[i]))
        nfin = row(norm_final) if i == depth - 1 else None
        ssm_par = (ssm_a_re[i], ssm_a_im[i], ssm_log_dt[i], ssm_b_re[i], ssm_b_im[i], ssm_c_re[i], ssm_c_im[i])
        return pre, post, nfin, ssm_par

    k_news, v_news, rss, iss = [], [], [], []
    for i in range(depth):
        pre, post, nfin, ssm_par = layer_weights(i)
        x1, q, _, _, kf, vf, uf, uc = _premix(xs, pre, perms_s, tm_s, tm_s, tm_s, (), dec_seq)
        att = _attn_sample(q, kf[0], vf[0], cache_kt, cache_vt, i, tab_s, dec_batch, dec_seq)
        h0 = _state_to_tiles(state_ssm_re[i], state_ssm_im[i], 1)
        y, h_last = _ssm(uc, h0, _ssm_tables(*ssm_par, dec_seq), dec_seq, dec_batch, False)
        xs = _postmix(x1, [(att, None)], (1,), y, uf, p_sample.reshape(depth, tm_s, D_PLE), i, post, perms_st,
                      nfin, tm_s, tm_s, dec_seq)
        k_news.append(kf[0])
        v_news.append(vf[0])
        re, im = _tiles_to_state(h_last)
        rss.append(re)
        iss.append(im)

    kp, vp, rp, ip = [], [], [], []
    for i in range(depth):
        pre, post, nfin, ssm_par = layer_weights(i)
        outs = _premix(xp, pre, perms_p, tm_p, seq, keep, dils[1:], chunk_p)
        x1, q, k, v = outs[:4]
        kf, vf, uf, uc = outs[-4:]
        nat = lambda a: a.reshape(batch, 1, seq, D_ATT)
        qkv = [(nat(q), nat(k), nat(v))] + [tuple(outs[4 + 3 * n:7 + 3 * n]) for n in range(len(dils) - 1)]
        planes = [_attn_prompt_branch(*qkv[bi], tabs_p[bi]) for bi in range(len(dils))]
        zeros = jnp.zeros((batch, N_GROUPS, SSM_STATE), F32)
        y, h_last = _ssm(uc, _state_to_tiles(zeros, zeros, batch), _ssm_tables(*ssm_par, chunk_p),
                         chunk_p, seq // chunk_p, True)
        xp = _postmix(x1, planes, dils, y, uf, p_prompt.reshape(depth, batch * seq, D_PLE), i, post, perms_pt,
                      nfin, tm_p, seq, chunk_p)
        kp.append(kf)
        vp.append(vf)
        re, im = _tiles_to_state(h_last)
        rp.append(re)
        ip.append(im)

    return (xp.reshape(batch, seq, D_MODEL), xs.reshape(dec_batch, dec_seq, D_MODEL),
            from_lanes(jnp.stack(kp)), from_lanes(jnp.stack(vp)), jnp.stack(rp), jnp.stack(ip),
            from_lanes(_window_update(cache_kt, jnp.stack(k_news), dec_seq)),
            from_lanes(_window_update(cache_vt, jnp.stack(v_news), dec_seq)),
            jnp.stack(rss), jnp.stack(iss))
```

```python
import functools
import math

import numpy as np
import jax
import jax.numpy as jnp
from jax import lax
from jax.experimental import pallas as pl
from jax.experimental.pallas import tpu as pltpu

F32 = jnp.float32
BF16 = jnp.bfloat16

D_MODEL = 1024
D_ATT = 512
D_SSM = 512
HEAD_DIM = 64
N_HEADS = 8
SSM_GROUP = 16
N_GROUPS = 32
SSM_STATE = 64
BRANCHES = ((128, 1), (512, 4), (2048, 16))
N_STEP = 128
N_BUCKETS = 32
MAX_DISTANCE = 2048
D_FF = 2816
D_PLE = 256
D_IN = 3 * D_ATT + D_SSM
EPS = 1e-6
NEG = -1e30

LANE = 128
MXU = 256
GROUPS_PER_TILE = LANE // SSM_GROUP
N_JT = D_SSM // LANE
PLANE = GROUPS_PER_TILE * SSM_STATE
TOK_PER_PAIR = MXU // LANE
VMEM_LIMIT = 56 * 1024 * 1024

assert all(w // d == N_STEP for w, d in BRANCHES)


def _rms(x, g):
    return x * lax.rsqrt(jnp.mean(x * x, axis=-1, keepdims=True) + EPS) * g


def _dot(a, b):
    return jnp.dot(a, b, preferred_element_type=F32)


def _ffn(x, g, wg_ref, wu_ref, wd_ref):
    h = _rms(x, g).astype(BF16)
    act = (jax.nn.silu(_dot(h, wg_ref[...])) * _dot(h, wu_ref[...])).astype(BF16)
    return x + 0.5 * _dot(act, wd_ref[...])


def _perm_matrix(n, d):
    p = np.zeros((n, n), np.float32)
    s, r = np.meshgrid(np.arange(n // d), np.arange(d), indexing="ij")
    p[(r * (n // d) + s).ravel(), (d * s + r).ravel()] = 1.0
    return p


def _permute_f32(pt_ref, x):
    hi = x.astype(BF16)
    rest = x - hi.astype(F32)
    mid = rest.astype(BF16)
    lo = (rest - mid.astype(F32)).astype(BF16)
    pt = pt_ref[...]
    return _dot(pt, hi) + _dot(pt, mid) + _dot(pt, lo)


def _const_spec(shape):
    nd = len(shape)
    return pl.BlockSpec(shape, lambda *_: (0,) * nd, pipeline_mode=pl.Buffered(1))


def _params(n_axes):
    return pltpu.CompilerParams(dimension_semantics=("arbitrary",) * n_axes,
                                vmem_limit_bytes=VMEM_LIMIT)


def _premix_kernel(*refs, dils, chunk, per_seq, skip):
    it = iter(refs)
    x_ref, nf_ref, wg_ref, wu_ref, wd_ref, nm_ref, win_ref = (next(it) for _ in range(7))
    perm = {d: next(it) for d in sorted(set(dils) | {chunk})}
    x1_ref, q_ref, k_ref, v_ref = (next(it) for _ in range(4))
    strided = {d: (next(it), next(it), next(it)) for d in dils}
    kf_ref, vf_ref, uf_ref, uc_ref = (next(it) for _ in range(4))

    x1 = _ffn(x_ref[...], nf_ref[...], wg_ref, wu_ref, wd_ref)
    x1_ref[...] = x1
    h = _rms(x1, nm_ref[...]).astype(BF16)
    qkvu = _dot(h, win_ref[...])
    q = (qkvu[:, :D_ATT] * (1.0 / math.sqrt(HEAD_DIM))).astype(BF16)
    kf = qkvu[:, D_ATT:2 * D_ATT]
    vf = qkvu[:, 2 * D_ATT:3 * D_ATT]
    uf = qkvu[:, 3 * D_ATT:]
    k, v, u = kf.astype(BF16), vf.astype(BF16), uf.astype(BF16)
    q_ref[...], k_ref[...], v_ref[...] = q, k, v
    uf_ref[...] = uf
    tm = x1.shape[0]

    @pl.when(pl.program_id(0) % per_seq >= skip)
    def _():
        kf_ref[0] = kf.T.reshape(N_HEADS, HEAD_DIM, tm)
        vf_ref[0] = vf.T.reshape(N_HEADS, HEAD_DIM, tm)

    for d, outs in strided.items():
        for val, out in zip((q, k, v), outs):
            out[0] = _dot(perm[d][...], val).astype(BF16).reshape(d, tm // d, D_ATT)
    up = _dot(perm[chunk][...], u).astype(BF16)
    rows = tm // chunk
    for t in range(chunk):
        for j in range(N_JT):
            uc_ref[j, :, t * LANE:(t + 1) * LANE] = up[t * rows:(t + 1) * rows, j * LANE:(j + 1) * LANE]


def _premix(x, weights, perms, tm, seq, keep, dils, chunk):
    m = x.shape[0]
    per_seq = seq // tm
    skip = (seq - keep) // tm
    assert keep % tm == 0 and tm % LANE == 0
    kept = pl.BlockSpec((1, N_HEADS, HEAD_DIM, tm),
                        lambda i: (i // per_seq, 0, 0, jnp.maximum(i % per_seq - skip, 0)))
    kept_shape = jax.ShapeDtypeStruct((m // seq, N_HEADS, HEAD_DIM, keep), F32)
    row = lambda w: pl.BlockSpec((tm, w), lambda i: (i, 0))
    consts = list(weights) + [perms[d] for d in sorted(set(dils) | {chunk})]
    in_specs = [row(D_MODEL)] + [_const_spec(c.shape) for c in consts]
    out_specs = [row(D_MODEL), row(D_ATT), row(D_ATT), row(D_ATT)]
    out_shape = [jax.ShapeDtypeStruct((m, D_MODEL), F32)] + [jax.ShapeDtypeStruct((m, D_ATT), BF16)] * 3
    for d in dils:
        out_specs += [pl.BlockSpec((1, d, tm // d, D_ATT), lambda i: (i // per_seq, 0, i % per_seq, 0))] * 3
        out_shape += [jax.ShapeDtypeStruct((m // seq, d, seq // d, D_ATT), BF16)] * 3
    out_specs += [kept, kept, row(D_SSM), pl.BlockSpec((N_JT, tm // chunk, chunk * LANE), lambda i: (0, i, 0))]
    out_shape += [kept_shape, kept_shape, jax.ShapeDtypeStruct((m, D_SSM), F32),
                  jax.ShapeDtypeStruct((N_JT, m // chunk, chunk * LANE), BF16)]
    return pl.pallas_call(
        functools.partial(_premix_kernel, dils=tuple(dils), chunk=chunk, per_seq=per_seq, skip=skip),
        grid=(m // tm,),
        in_specs=in_specs, out_specs=out_specs, out_shape=out_shape,
        compiler_params=_params(1),
        name="premix",
    )(x, *consts)


def _attn_prompt_kernel(q_ref, kc_ref, kp_ref, vc_ref, vp_ref, bias_ref, o_ref, l_ref, kbuf, vbuf, tab_ref,
                        *, qb):
    first = pl.program_id(2) == 0
    for h in range(N_HEADS):
        row = jnp.broadcast_to(bias_ref[h:h + 1, :], (N_STEP, 2 * N_STEP))
        tab_ref[h] = pltpu.roll(row, 0, 1, stride=1, stride_axis=0)
    kbuf[0:N_STEP, :] = kp_ref[0, 0]
    kbuf[N_STEP:, :] = kc_ref[0, 0]
    vbuf[0:N_STEP, :] = vp_ref[0, 0]
    vbuf[N_STEP:, :] = vc_ref[0, 0]
    nk = 2 * N_STEP
    lo_half = lax.broadcasted_iota(jnp.int32, (N_STEP, LANE), 1) < HEAD_DIM
    keep_lo = jnp.where(lo_half, 1.0, 0.0).astype(BF16)
    keep_hi = jnp.where(lo_half, 0.0, 1.0).astype(BF16)
    col = lax.broadcasted_iota(jnp.int32, (N_STEP, nk), 1)
    before_start = jnp.where((col < N_STEP) & first, NEG, 0.0)

    def block(qi, carry):
        r0 = pl.multiple_of(qi * N_STEP, N_STEP)
        edge = jnp.where(qi == 0, before_start, 0.0)
        for hp in range(N_HEADS // 2):
            lanes = slice(hp * LANE, (hp + 1) * LANE)
            qp = q_ref[0, 0, pl.ds(r0, N_STEP), lanes]
            kp = kbuf[pl.ds(r0, nk), lanes]
            vp = vbuf[pl.ds(r0, nk), lanes]
            o_pair, l_pair = None, None
            for hh in range(2):
                qm = qp * (keep_lo if hh == 0 else keep_hi)
                s = lax.dot_general(qm, kp, (((1,), (1,)), ((), ())), preferred_element_type=F32)
                s = s + tab_ref[2 * hp + hh] + edge
                m = jnp.max(s, axis=-1, keepdims=True)
                e = jnp.exp(s - m)
                den = jnp.sum(e, axis=-1, keepdims=True)
                o = _dot(e.astype(BF16), vp) / den
                lse = jnp.broadcast_to(m + jnp.log(den), (N_STEP, LANE))
                if hh == 0:
                    o_pair, l_pair = o, lse
                else:
                    o_pair = jnp.where(lo_half, o_pair, o)
                    l_pair = jnp.where(lo_half, l_pair, lse)
            o_ref[0, 0, pl.ds(r0, N_STEP), lanes] = o_pair
            l_ref[0, 0, pl.ds(r0, N_STEP), lanes] = l_pair
        return carry

    lax.fori_loop(0, qb // N_STEP, block, 0)


def _attn_prompt_branch(q, k, v, bias_row):
    batch, dil, sub, _ = q.shape
    qb = min(sub, 512)
    assert sub % qb == 0 and qb % N_STEP == 0
    per = qb // N_STEP
    cur = pl.BlockSpec((1, 1, qb, D_ATT), lambda b, r, i: (b, r, i, 0))
    prev = pl.BlockSpec((1, 1, N_STEP, D_ATT), lambda b, r, i: (b, r, jnp.maximum(i * per - 1, 0), 0))
    return pl.pallas_call(
        functools.partial(_attn_prompt_kernel, qb=qb),
        grid=(batch, dil, sub // qb),
        in_specs=[cur, cur, prev, cur, prev, _const_spec(bias_row.shape)],
        out_specs=[cur, cur],
        out_shape=[jax.ShapeDtypeStruct(q.shape, F32)] * 2,
        scratch_shapes=[pltpu.VMEM((qb + N_STEP, D_ATT), BF16)] * 2
                       + [pltpu.VMEM((N_HEADS, N_STEP, 2 * N_STEP), F32)],
        compiler_params=_params(3),
        name=f"attn_prompt_d{dil}",
    )(q, k, k, v, v, bias_row)


Q_PAD = 8


def _sample_steps(cache_len, t_new, n_new):
    step_of = np.full((len(BRANCHES), t_new, cache_len + n_new), -1, np.int32)
    for bi, (w, d) in enumerate(BRANCHES):
        for i in range(t_new):
            for j in range(w // d + 1):
                ext = cache_len + i - j * d
                assert ext >= 0
                step_of[bi, i, ext] = j
    return step_of


def _attn_sample_kernel(q_ref, kn_ref, vn_ref, kc_ref, vc_ref, tab_ref, o_ref, qs, *, t_new):
    n_new = kn_ref.shape[2]
    to_front = (n_new - pl.program_id(0) * t_new) % n_new
    q = q_ref[0].astype(F32)
    qs[...] = jnp.zeros(qs.shape, F32)
    for h in range(N_HEADS):
        qs[h, 0:t_new, :] = q[:, h * HEAD_DIM:(h + 1) * HEAD_DIM]
        qh = qs[h].astype(BF16)
        kn = pltpu.roll(kn_ref[h], to_front, 1).astype(BF16)
        vn = pltpu.roll(vn_ref[h], to_front, 1).astype(BF16)
        s = jnp.concatenate([_dot(qh, kc_ref[0, h].astype(BF16)), _dot(qh, kn)], axis=-1)
        logits = [s + tab_ref[bi, h] for bi in range(len(BRANCHES))]
        m = jnp.max(functools.reduce(jnp.maximum, logits), axis=-1, keepdims=True)
        e = sum(jnp.exp(l - m) for l in logits)
        den = jnp.sum(e, axis=-1, keepdims=True)
        eb = e.astype(BF16)
        nt = (((1,), (1,)), ((), ()))
        o = (lax.dot_general(eb[:, :-n_new], vc_ref[0, h].astype(BF16), nt, preferred_element_type=F32)
             + lax.dot_general(eb[:, -n_new:], vn, nt, preferred_element_type=F32)) / den
        o_ref[0, :, h * HEAD_DIM:(h + 1) * HEAD_DIM] = o[:t_new]


def _attn_sample(q, k_new, v_new, cache_k, cache_v, layer, tab, batch, t_new):
    cache_len = cache_k.shape[-1]
    tok = pl.BlockSpec((1, t_new, D_ATT), lambda b: (b, 0, 0))
    slab = pl.BlockSpec((pl.Squeezed(), 1, N_HEADS, HEAD_DIM, cache_len), lambda b: (layer, b, 0, 0, 0))
    out = pl.pallas_call(
        functools.partial(_attn_sample_kernel, t_new=t_new),
        grid=(batch,),
        in_specs=[tok, _const_spec(k_new.shape), _const_spec(v_new.shape), slab, slab, _const_spec(tab.shape)],
        out_specs=tok,
        out_shape=jax.ShapeDtypeStruct((batch, t_new, D_ATT), F32),
        scratch_shapes=[pltpu.VMEM((N_HEADS, Q_PAD, HEAD_DIM), F32)],
        compiler_params=_params(1),
        name="attn_sample",
    )(q.reshape(batch, t_new, D_ATT), k_new, v_new, cache_k, cache_v, tab)
    return out.reshape(batch * t_new, D_ATT)


def _ssm_kernel(u_ref, h0_ref, w_ref, pw_ref, qw_ref, a_ref, y_ref, hl_ref, z_buf, hb_buf,
                *, chunk, scan_rows):
    n_pair = chunk // TOK_PER_PAIR
    ub = u_ref[0]
    z = _dot(ub, pw_ref[0])
    ar = a_ref[0, 0:1, :]
    ai = a_ref[0, 1:2, :]
    h0 = h0_ref[0, 0]
    if scan_rows:
        z_buf[...] = z

        def step(r, h):
            hr, hi = h
            hb_buf[pl.ds(r, 1), 0:PLANE] = hr
            hb_buf[pl.ds(r, 1), PLANE:] = hi
            zr = z_buf[pl.ds(r, 1), 0:PLANE]
            zi = z_buf[pl.ds(r, 1), PLANE:]
            return ar * hr - ai * hi + zr, ar * hi + ai * hr + zi

        hr, hi = lax.fori_loop(0, ub.shape[0], step, (h0[:, 0:PLANE], h0[:, PLANE:]))
        hl_ref[0, 0, :, 0:PLANE] = hr
        hl_ref[0, 0, :, PLANE:] = hi
        hb = hb_buf[...].astype(BF16)
    else:
        hr, hi = h0[:, 0:PLANE], h0[:, PLANE:]
        hl_ref[0, 0, :, 0:PLANE] = ar * hr - ai * hi + z[:, 0:PLANE]
        hl_ref[0, 0, :, PLANE:] = ar * hi + ai * hr + z[:, PLANE:]
        hb = h0.astype(BF16)
    for b in range(n_pair):
        acc = _dot(hb, qw_ref[0, b])
        for a in range(b + 1):
            acc = acc + _dot(ub[:, a * MXU:(a + 1) * MXU], w_ref[0, b - a])
        y_ref[0, :, b * MXU:(b + 1) * MXU] = acc


def _ssm(uc, h0, tabs, chunk, rows_per_block, scan_rows):
    w, pw, qw, a_pow = tabs
    rows, width = uc.shape[1:]
    n_blocks = rows // rows_per_block
    seqs = h0.shape[2]
    return pl.pallas_call(
        functools.partial(_ssm_kernel, chunk=chunk, scan_rows=scan_rows),
        grid=(N_JT, n_blocks),
        in_specs=[pl.BlockSpec((1, rows_per_block, width), lambda j, b: (j, b, 0)),
                  pl.BlockSpec((1, 1, seqs, 2 * PLANE), lambda j, b: (j, b, 0, 0)),
                  pl.BlockSpec((1,) + w.shape[1:], lambda j, b: (j, 0, 0, 0)),
                  pl.BlockSpec((1,) + pw.shape[1:], lambda j, b: (j, 0, 0)),
                  pl.BlockSpec((1,) + qw.shape[1:], lambda j, b: (j, 0, 0, 0)),
                  pl.BlockSpec((1,) + a_pow.shape[1:], lambda j, b: (j, 0, 0))],
        out_specs=[pl.BlockSpec((1, rows_per_block, width), lambda j, b: (j, b, 0)),
                   pl.BlockSpec((1, 1, seqs, 2 * PLANE), lambda j, b: (j, b, 0, 0))],
        out_shape=[jax.ShapeDtypeStruct(uc.shape, F32),
                   jax.ShapeDtypeStruct(h0.shape, F32)],
        scratch_shapes=[pltpu.VMEM((rows_per_block, 2 * PLANE), F32)] * 2,
        compiler_params=_params(2),
        name=f"ssm_chunk{chunk}",
    )(uc, h0, w, pw, qw, a_pow)


def _ssm_tables(a_re, a_im, log_dt, b_re, b_im, c_re, c_im, chunk):
    hi = lax.Precision.HIGHEST
    dt = jnp.exp(log_dt)[:, None]
    den = a_re * a_re + a_im * a_im
    er, ang = jnp.exp(a_re * dt), a_im * dt
    ab_re, ab_im = er * jnp.cos(ang), er * jnp.sin(ang)
    nr, ni = ab_re - 1.0, ab_im
    f_re, f_im = (nr * a_re + ni * a_im) / den, (ni * a_re - nr * a_im) / den
    bb_re = f_re[..., None] * b_re - f_im[..., None] * b_im
    bb_im = f_re[..., None] * b_im + f_im[..., None] * b_re

    def power(taus):
        taus = jnp.asarray(taus, F32)[:, None, None]
        mag, ang_t = jnp.exp(a_re * dt * taus), a_im * dt * taus
        return mag * jnp.cos(ang_t), mag * jnp.sin(ang_t)

    p_re, p_im = power(np.arange(chunk + 1))
    pr, pi = (x[..., None] for x in power(chunk - 1 - np.arange(chunk)))
    s_re, s_im = pr * bb_re - pi * bb_im, pr * bb_im + pi * bb_re
    qr, qi = p_re[1:, :, None, :], p_im[1:, :, None, :]
    e_re, e_im = c_re * qr - c_im * qi, c_re * qi + c_im * qr
    lr, li = p_re[:chunk, :, None, :], p_im[:chunk, :, None, :]
    cl_re, cl_im = c_re * lr - c_im * li, c_re * li + c_im * lr
    kern = (jnp.einsum("tgon,gni->tgoi", cl_re, bb_re, precision=hi)
            - jnp.einsum("tgon,gni->tgoi", cl_im, bb_im, precision=hi))

    eye = jnp.eye(GROUPS_PER_TILE, dtype=F32)
    n_pair = chunk // TOK_PER_PAIR
    tile = lambda x: x.reshape((x.shape[0], N_JT, GROUPS_PER_TILE) + x.shape[2:])
    kern_t = tile(kern)
    kern_t = jnp.concatenate([kern_t, jnp.zeros_like(kern_t[:1])], axis=0)
    lag = (TOK_PER_PAIR * np.arange(n_pair)[:, None, None]
           + np.arange(TOK_PER_PAIR)[None, None, :] - np.arange(TOK_PER_PAIR)[None, :, None])
    kl = kern_t[lag]
    w = jnp.einsum("dpqjgoi,gh->jdpgiqho", kl, eye)
    w = w.reshape(N_JT, n_pair, MXU, MXU)
    gi = GROUPS_PER_TILE
    row_g = (np.arange(chunk * LANE) // SSM_GROUP) % gi
    mask_p = jnp.asarray(row_g[:, None] == (np.arange(PLANE) // SSM_STATE)[None], F32)

    def state_image(s):
        a = tile(s).transpose(1, 0, 2, 4, 3).reshape(N_JT, chunk * LANE, SSM_STATE)
        return jnp.tile(a, (1, 1, gi)) * mask_p

    pw = jnp.concatenate([state_image(s_re), state_image(s_im)], axis=-1)
    mask_q = jnp.asarray((np.arange(PLANE) // SSM_STATE)[:, None] == (np.arange(LANE) // SSM_GROUP)[None], F32)

    def readout(e):
        a = tile(e).transpose(1, 0, 2, 4, 3)
        a = a.reshape(N_JT, n_pair, TOK_PER_PAIR, gi, SSM_STATE, SSM_GROUP).transpose(0, 1, 3, 4, 2, 5)
        a = a.reshape(N_JT, n_pair, PLANE, TOK_PER_PAIR, SSM_GROUP)
        return jnp.concatenate([jnp.tile(a[:, :, :, t], (1, 1, 1, gi)) * mask_q
                                for t in range(TOK_PER_PAIR)], axis=-1)

    qw = jnp.concatenate([readout(e_re), readout(-e_im)], axis=2)
    a_pow = jnp.stack([p_re[chunk].reshape(N_JT, PLANE), p_im[chunk].reshape(N_JT, PLANE)], axis=1)
    return w.astype(BF16), pw.astype(BF16), qw.astype(BF16), a_pow


def _state_to_tiles(re, im, n_blocks):
    n_seq = re.shape[0]
    t = lambda x: x.reshape(n_blocks, n_seq // n_blocks, N_JT, PLANE).transpose(2, 0, 1, 3)
    return jnp.concatenate([t(re), t(im)], axis=-1)


def _tiles_to_state(h):
    n_seq = h.shape[1] * h.shape[2]
    t = lambda x: x.transpose(1, 2, 0, 3).reshape(n_seq, N_GROUPS, SSM_STATE)
    return t(h[..., :PLANE]), t(h[..., PLANE:])


def _postmix_kernel(*refs, dils, chunk, final):
    it = iter(refs)
    x1_ref = next(it)
    att_refs = [(next(it), next(it)) for _ in dils] if len(dils) > 1 else [(next(it), None)]
    y_ref, uf_ref, p_ref = next(it), next(it), next(it)
    (natt_ref, nssm_ref, dskip_ref, wglu_ref, bglu_ref, wout_ref, nf_ref, wg_ref, wu_ref, wd_ref,
     nple_ref, wpg_ref, wpp_ref) = (next(it) for _ in range(13))
    perm_t = {d: next(it) for d in sorted((set(dils) | {chunk}) - {1})}
    nfin_ref = next(it) if final else None
    out_ref, ybuf = next(it), next(it)
    tm = x1_ref.shape[0]

    def natural(ref, d):
        val = ref[...] if d == 1 else ref[0].reshape(tm, D_ATT)
        return val if d == 1 else _permute_f32(perm_t[d], val)

    if len(dils) == 1:
        att = att_refs[0][0][...]
    else:
        outs = [natural(o, d) for d, (o, _) in zip(dils, att_refs)]
        lses = [natural(l, d) for d, (_, l) in zip(dils, att_refs)]
        top = functools.reduce(jnp.maximum, lses)
        ws = [jnp.exp(l - top) for l in lses]
        att = sum(w * o for w, o in zip(ws, outs)) / sum(ws)
    a_n = _rms(att, natt_ref[...])
    rows = tm // chunk
    for t in range(chunk):
        for j in range(N_JT):
            ybuf[t * rows:(t + 1) * rows, j * LANE:(j + 1) * LANE] = y_ref[j, :, t * LANE:(t + 1) * LANE]
    y = _permute_f32(perm_t[chunk], ybuf[...]) + dskip_ref[...] * uf_ref[...]
    z = jax.nn.gelu(y)
    s = z * jax.nn.sigmoid(_dot(z.astype(BF16), wglu_ref[...]) + bglu_ref[...])
    s_n = _rms(s, nssm_ref[...])
    mixed = jnp.concatenate([a_n, s_n], axis=-1).astype(BF16)
    x2 = x1_ref[...] + _dot(mixed, wout_ref[...])
    x3 = _ffn(x2, nf_ref[...], wg_ref, wu_ref, wd_ref)
    gate = jax.nn.sigmoid(_dot(_rms(x3, nple_ref[...]).astype(BF16), wpg_ref[...]))
    x4 = x3 + gate * _dot(p_ref[...].astype(BF16), wpp_ref[...])
    out_ref[...] = _rms(x4, nfin_ref[...]) if final else x4


def _postmix(x1, att_planes, dils, y, uf, p, layer, weights, perms_t, nfin, tm, seq, chunk):
    m = x1.shape[0]
    per_seq = seq // tm
    final = nfin is not None
    row = lambda w: pl.BlockSpec((tm, w), lambda i: (i, 0))
    att_args, att_specs = [], []
    for d, (o, l) in zip(dils, att_planes):
        spec = row(D_ATT) if d == 1 else pl.BlockSpec((1, d, tm // d, D_ATT),
                                                       lambda i: (i // per_seq, 0, i % per_seq, 0))
        for a in ((o, l) if len(dils) > 1 else (o,)):
            att_args.append(a.reshape(m, D_ATT) if d == 1 else a)
            att_specs.append(spec)
    consts = list(weights) + [perms_t[d] for d in sorted((set(dils) | {chunk}) - {1})] + ([nfin] if final else [])
    args = [x1] + att_args + [y, uf, p] + consts
    in_specs = ([row(D_MODEL)] + att_specs
                + [pl.BlockSpec((N_JT, tm // chunk, chunk * LANE), lambda i: (0, i, 0)), row(D_SSM),
                   pl.BlockSpec((pl.Squeezed(), tm, D_PLE), lambda i: (layer, i, 0))]
                + [_const_spec(c.shape) for c in consts])
    return pl.pallas_call(
        functools.partial(_postmix_kernel, dils=tuple(dils), chunk=chunk, final=final),
        grid=(m // tm,),
        in_specs=in_specs,
        out_specs=row(D_MODEL),
        out_shape=jax.ShapeDtypeStruct((m, D_MODEL), F32),
        scratch_shapes=[pltpu.VMEM((tm, D_SSM), F32)],
        compiler_params=_params(1),
        name="postmix",
    )(*args)


def _window_kernel(old_ref, new_ref, out_ref, *, t_new):
    length, n_new = old_ref.shape[-1], new_ref.shape[-1]
    to_back = (LANE - t_new - pl.program_id(1) * t_new) % n_new
    is_new = lax.broadcasted_iota(jnp.int32, (HEAD_DIM, LANE), 1) >= LANE - t_new
    for h in range(N_HEADS):
        shifted = pltpu.roll(old_ref[0, 0, h], length - t_new, 1)
        new = pltpu.roll(new_ref[0, h], to_back, 1)[:, :LANE]
        out_ref[0, 0, h, :, :length - LANE] = shifted[:, :length - LANE]
        out_ref[0, 0, h, :, length - LANE:] = jnp.where(is_new, new, shifted[:, length - LANE:])


def _window_update(old, new, t_new):
    depth, batch = old.shape[:2]
    assert new.shape[-1] == batch * t_new and new.shape[-1] % LANE == 0 and t_new <= LANE
    slab = pl.BlockSpec((1, 1) + old.shape[2:], lambda l, b: (l, b, 0, 0, 0))
    return pl.pallas_call(
        functools.partial(_window_kernel, t_new=t_new),
        grid=(depth, batch),
        in_specs=[slab, pl.BlockSpec((1,) + new.shape[1:], lambda l, b: (l, 0, 0, 0))],
        out_specs=slab,
        out_shape=jax.ShapeDtypeStruct(old.shape, old.dtype),
        compiler_params=_params(2),
        name="window_update",
    )(old, new)


def _t5_bucket(dist):
    dist = np.asarray(dist, dtype=np.int64)
    exact = N_BUCKETS // 2
    ratio = np.log(np.maximum(dist, 1) / exact) / np.log(MAX_DISTANCE / exact)
    large = np.minimum(exact + (ratio * (N_BUCKETS - exact)).astype(np.int64), N_BUCKETS - 1)
    return np.where(dist < exact, dist, large).astype(np.int32)


def _branch_bias(rel_bias, dil):
    return rel_bias[_t5_bucket(np.arange(N_STEP + 1) * dil)].T.astype(F32)


def _prompt_bias_row(rel_bias, dil):
    step = N_STEP - np.arange(2 * N_STEP)
    row = rel_bias[_t5_bucket(np.maximum(step, 0) * dil)].T.astype(F32)
    return jnp.where((step >= 0)[None], row, NEG)


def _sample_table(biases, step_of):
    n_br, t_new, cols = step_of.shape
    tabs = []
    for bi in range(n_br):
        vals = biases[bi][:, np.maximum(step_of[bi], 0)]
        tab = jnp.where((step_of[bi] >= 0)[None], vals, NEG)
        tabs.append(jnp.concatenate([tab, jnp.zeros((N_HEADS, Q_PAD - t_new, cols), F32)], axis=1))
    return jnp.stack(tabs)


def kernel(x_prompt, x_sample, p_prompt, p_sample, cache_k, cache_v, state_ssm_re, state_ssm_im,
           rel_bias, w_in, w_out, norm_mix, norm_att_out, norm_ssm_out, norm_ffn, ffn_w_gate,
           ffn_w_up, ffn_w_down, ssm_a_re, ssm_a_im, ssm_log_dt, ssm_b_re, ssm_b_im, ssm_c_re,
           ssm_c_im, ssm_d, w_glu, b_glu, norm_ple, w_ple_gate, w_ple_proj, norm_final):
    depth = w_in.shape[0]
    batch, seq, _ = x_prompt.shape
    dec_batch, dec_seq, _ = x_sample.shape
    cache_len = cache_k.shape[2]
    keep = min(BRANCHES[-1][0], seq)
    dils = tuple(d for _, d in BRANCHES)
    chunk_p = dils[-1]
    tm_p = 256
    tm_s = dec_batch * dec_seq
    assert seq % tm_p == 0 and chunk_p in dils and dec_seq % TOK_PER_PAIR == 0 and tm_s % 8 == 0

    xp = x_prompt.reshape(batch * seq, D_MODEL)
    xs = x_sample.reshape(tm_s, D_MODEL)
    biases = [_branch_bias(rel_bias, d) for d in dils]
    tabs_p = [_prompt_bias_row(rel_bias, d) for d in dils]
    tab_s = _sample_table(biases, _sample_steps(cache_len, dec_seq, tm_s))
    to_lanes = lambda a: a.transpose(0, 1, 3, 4, 2)
    from_lanes = lambda a: a.transpose(0, 1, 4, 2, 3)
    cache_kt, cache_vt = to_lanes(cache_k), to_lanes(cache_v)
    perms_p = {d: jnp.asarray(_perm_matrix(tm_p, d), BF16) for d in dils if d > 1}
    perms_pt = {d: jnp.asarray(_perm_matrix(tm_p, d).T, BF16) for d in dils if d > 1}
    perms_s = {dec_seq: jnp.asarray(_perm_matrix(tm_s, dec_seq), BF16)}
    perms_st = {dec_seq: jnp.asarray(_perm_matrix(tm_s, dec_seq).T, BF16)}
    row = lambda v: v.reshape(1, -1)
    bf = lambda a: a.astype(BF16)

    def layer_weights(i):
        pre = (row(norm_ffn[i, 0]), bf(ffn_w_gate[i, 0]), bf(ffn_w_up[i, 0]), bf(ffn_w_down[i, 0]),
               row(norm_mix[i]), bf(w_in[i]))
        post = (row(norm_att_out[i]), row(norm_ssm_out[i]), row(ssm_d[i]), bf(w_glu[i]), row(b_glu[i]),
                bf(w_out[i]), row(norm_ffn[i, 1]), bf(ffn_w_gate[i, 1]), bf(ffn_w_up[i, 1]),
                bf(ffn_w_down[i, 1]), row(norm_ple[i]), bf(w_ple_gate[i]), bf(w_ple_proj[i]))
        nfin = row(norm_final) if i == depth - 1 else None
        ssm_par = (ssm_a_re[i], ssm_a_im[i], ssm_log_dt[i], ssm_b_re[i], ssm_b_im[i], ssm_c_re[i], ssm_c_im[i])
        return pre, post, nfin, ssm_par

    k_news, v_news, rss, iss = [], [], [], []
    for i in range(depth):
        pre, post, nfin, ssm_par = layer_weights(i)
        x1, q, _, _, kf, vf, uf, uc = _premix(xs, pre, perms_s, tm_s, tm_s, tm_s, (), dec_seq)
        att = _attn_sample(q, kf[0], vf[0], cache_kt, cache_vt, i, tab_s, dec_batch, dec_seq)
        h0 = _state_to_tiles(state_ssm_re[i], state_ssm_im[i], 1)
        y, h_last = _ssm(uc, h0, _ssm_tables(*ssm_par, dec_seq), dec_seq, dec_batch, False)
        xs = _postmix(x1, [(att, None)], (1,), y, uf, p_sample.reshape(depth, tm_s, D_PLE), i, post, perms_st,
                      nfin, tm_s, tm_s, dec_seq)
        k_news.append(kf[0])
        v_news.append(vf[0])
        re, im = _tiles_to_state(h_last)
        rss.append(re)
        iss.append(im)

    kp, vp, rp, ip = [], [], [], []
    for i in range(depth):
        pre, post, nfin, ssm_par = layer_weights(i)
        outs = _premix(xp, pre, perms_p, tm_p, seq, keep, dils[1:], chunk_p)
        x1, q, k, v = outs[:4]
        kf, vf, uf, uc = outs[-4:]
        nat = lambda a: a.reshape(batch, 1, seq, D_ATT)
        qkv = [(nat(q), nat(k), nat(v))] + [tuple(outs[4 + 3 * n:7 + 3 * n]) for n in range(len(dils) - 1)]
        planes = [_attn_prompt_branch(*qkv[bi], tabs_p[bi]) for bi in range(len(dils))]
        zeros = jnp.zeros((batch, N_GROUPS, SSM_STATE), F32)
        y, h_last = _ssm(uc, _state_to_tiles(zeros, zeros, batch), _ssm_tables(*ssm_par, chunk_p),
                         chunk_p, seq // chunk_p, True)
        xp = _postmix(x1, planes, dils, y, uf, p_prompt.reshape(depth, batch * seq, D_PLE), i, post, perms_pt,
                      nfin, tm_p, seq, chunk_p)
        kp.append(kf)
        vp.append(vf)
        re, im = _tiles_to_state(h_last)
        rp.append(re)
        ip.append(im)

    return (xp.reshape(batch, seq, D_MODEL), xs.reshape(dec_batch, dec_seq, D_MODEL),
            from_lanes(jnp.stack(kp)), from_lanes(jnp.stack(vp)), jnp.stack(rp), jnp.stack(ip),
            from_lanes(_window_update(cache_kt, jnp.stack(k_news), dec_seq)),
            from_lanes(_window_update(cache_vt, jnp.stack(v_news), dec_seq)),
            jnp.stack(rss), jnp.stack(iss))
```

```python
import functools
import math

import numpy as np
import jax
import jax.numpy as jnp
from jax import lax
from jax.experimental import pallas as pl
from jax.experimental.pallas import tpu as pltpu

F32 = jnp.float32
BF16 = jnp.bfloat16

D_MODEL = 1024
D_ATT = 512
D_SSM = 512
HEAD_DIM = 64
N_HEADS = 8
SSM_GROUP = 16
N_GROUPS = 32
SSM_STATE = 64
BRANCHES = ((128, 1), (512, 4), (2048, 16))
N_STEP = 128
N_BUCKETS = 32
MAX_DISTANCE = 2048
D_FF = 2816
D_PLE = 256
D_IN = 3 * D_ATT + D_SSM
EPS = 1e-6
NEG = -1e30

LANE = 128
MXU = 256
GROUPS_PER_TILE = LANE // SSM_GROUP
N_JT = D_SSM // LANE
PLANE = GROUPS_PER_TILE * SSM_STATE
TOK_PER_PAIR = MXU // LANE
VMEM_LIMIT = 56 * 1024 * 1024

assert all(w // d == N_STEP for w, d in BRANCHES)


def _rms(x, g):
    return x * lax.rsqrt(jnp.mean(x * x, axis=-1, keepdims=True) + EPS) * g


def _dot(a, b):
    return jnp.dot(a, b, preferred_element_type=F32)


def _ffn(x, g, wg_ref, wu_ref, wd_ref):
    h = _rms(x, g).astype(BF16)
    act = (jax.nn.silu(_dot(h, wg_ref[...])) * _dot(h, wu_ref[...])).astype(BF16)
    return x + 0.5 * _dot(act, wd_ref[...])


def _perm_matrix(n, d):
    p = np.zeros((n, n), np.float32)
    s, r = np.meshgrid(np.arange(n // d), np.arange(d), indexing="ij")
    p[(r * (n // d) + s).ravel(), (d * s + r).ravel()] = 1.0
    return p


def _permute_f32(pt_ref, x):
    hi = x.astype(BF16)
    rest = x - hi.astype(F32)
    mid = rest.astype(BF16)
    lo = (rest - mid.astype(F32)).astype(BF16)
    pt = pt_ref[...]
    return _dot(pt, hi) + _dot(pt, mid) + _dot(pt, lo)


def _const_spec(shape):
    nd = len(shape)
    return pl.BlockSpec(shape, lambda *_: (0,) * nd, pipeline_mode=pl.Buffered(1))


def _params(n_axes):
    return pltpu.CompilerParams(dimension_semantics=("arbitrary",) * n_axes,
                                vmem_limit_bytes=VMEM_LIMIT)


def _window_part(old_ref, next_ref, new_ref, out_ref, part, n_parts, seq_idx, t_new):
    width, n_new = old_ref.shape[-1], new_ref.shape[-1]
    to_back = (LANE - t_new - seq_idx * t_new) % n_new
    tail = lax.broadcasted_iota(jnp.int32, (HEAD_DIM, LANE), 1) >= LANE - t_new
    for h in range(N_HEADS):
        shifted = pltpu.roll(old_ref[0, 0, h], width - t_new, 1)
        incoming = jnp.where(part == n_parts - 1,
                             pltpu.roll(new_ref[0, h], to_back, 1)[:, :LANE],
                             pltpu.roll(next_ref[0, 0, h], LANE - t_new, 1))
        out_ref[0, 0, h, :, :width - LANE] = shifted[:, :width - LANE]
        out_ref[0, 0, h, :, width - LANE:] = jnp.where(tail, incoming, shifted[:, width - LANE:])


def _premix_kernel(*refs, dils, chunk, per_seq, skip, window):
    it = iter(refs)
    x_ref, nf_ref, wg_ref, wu_ref, wd_ref, nm_ref, win_ref = (next(it) for _ in range(7))
    perm = {d: next(it) for d in sorted(set(dils) | {chunk})}
    window_in = [next(it) for _ in range(3)] if window else None
    x1_ref, q_ref, k_ref, v_ref = (next(it) for _ in range(4))
    strided = {d: (next(it), next(it), next(it)) for d in dils}
    kf_ref, vf_ref, uf_ref, uc_ref = (next(it) for _ in range(4))
    if window:
        n_parts, n_seq, t_new = window
        step = pl.program_id(0)
        _window_part(*window_in, next(it), step % n_parts, n_parts, (step // n_parts) % n_seq, t_new)

    x1 = _ffn(x_ref[...], nf_ref[...], wg_ref, wu_ref, wd_ref)
    x1_ref[...] = x1
    h = _rms(x1, nm_ref[...]).astype(BF16)
    qkvu = _dot(h, win_ref[...])
    q = (qkvu[:, :D_ATT] * (1.0 / math.sqrt(HEAD_DIM))).astype(BF16)
    kf = qkvu[:, D_ATT:2 * D_ATT]
    vf = qkvu[:, 2 * D_ATT:3 * D_ATT]
    uf = qkvu[:, 3 * D_ATT:]
    k, v, u = kf.astype(BF16), vf.astype(BF16), uf.astype(BF16)
    q_ref[...], k_ref[...], v_ref[...] = q, k, v
    uf_ref[...] = uf
    tm = x1.shape[0]

    @pl.when(pl.program_id(0) % per_seq >= skip)
    def _():
        kf_ref[0] = kf.T.reshape(N_HEADS, HEAD_DIM, tm)
        vf_ref[0] = vf.T.reshape(N_HEADS, HEAD_DIM, tm)

    for d, outs in strided.items():
        for val, out in zip((q, k, v), outs):
            out[0] = _dot(perm[d][...], val).astype(BF16).reshape(d, tm // d, D_ATT)
    up = _dot(perm[chunk][...], u).astype(BF16)
    rows = tm // chunk
    for t in range(chunk):
        for j in range(N_JT):
            uc_ref[j, :, t * LANE:(t + 1) * LANE] = up[t * rows:(t + 1) * rows, j * LANE:(j + 1) * LANE]


def _premix(x, weights, perms, tm, seq, keep, dils, chunk, window=None):
    m = x.shape[0]
    per_seq = seq // tm
    skip = (seq - keep) // tm
    assert keep % tm == 0 and tm % LANE == 0
    kept = pl.BlockSpec((1, N_HEADS, HEAD_DIM, tm),
                        lambda i: (i // per_seq, 0, 0, jnp.maximum(i % per_seq - skip, 0)))
    kept_shape = jax.ShapeDtypeStruct((m // seq, N_HEADS, HEAD_DIM, keep), F32)
    row = lambda w: pl.BlockSpec((tm, w), lambda i: (i, 0))
    consts = list(weights) + [perms[d] for d in sorted(set(dils) | {chunk})]
    in_specs = [row(D_MODEL)] + [_const_spec(c.shape) for c in consts]
    out_specs = [row(D_MODEL), row(D_ATT), row(D_ATT), row(D_ATT)]
    out_shape = [jax.ShapeDtypeStruct((m, D_MODEL), F32)] + [jax.ShapeDtypeStruct((m, D_ATT), BF16)] * 3
    for d in dils:
        out_specs += [pl.BlockSpec((1, d, tm // d, D_ATT), lambda i: (i // per_seq, 0, i % per_seq, 0))] * 3
        out_shape += [jax.ShapeDtypeStruct((m // seq, d, seq // d, D_ATT), BF16)] * 3
    out_specs += [kept, kept, row(D_SSM), pl.BlockSpec((N_JT, tm // chunk, chunk * LANE), lambda i: (0, i, 0))]
    out_shape += [kept_shape, kept_shape, jax.ShapeDtypeStruct((m, D_SSM), F32),
                  jax.ShapeDtypeStruct((N_JT, m // chunk, chunk * LANE), BF16)]
    args, window_cfg = [x] + consts, None
    if window is not None:
        old, new, t_new = window
        depth, n_seq, length = old.shape[0], old.shape[1], old.shape[-1]
        n_parts = (m // tm) // (depth * n_seq)
        width = length // n_parts
        assert n_parts * depth * n_seq == m // tm and width * n_parts == length and width % LANE == 0
        assert new.shape[-1] == n_seq * t_new and new.shape[-1] % LANE == 0 and t_new <= LANE
        tiles = width // LANE
        where = lambda i: (i // (n_parts * n_seq), (i // n_parts) % n_seq, i % n_parts)
        part = pl.BlockSpec((1, 1, N_HEADS, HEAD_DIM, width), lambda i: (*where(i)[:2], 0, 0, where(i)[2]))
        after = pl.BlockSpec((1, 1, N_HEADS, HEAD_DIM, LANE),
                             lambda i: (*where(i)[:2], 0, 0,
                                        jnp.minimum((where(i)[2] + 1) * tiles, length // LANE - 1)))
        args += [old, old, new]
        in_specs += [part, after, pl.BlockSpec((1,) + new.shape[1:], lambda i: (where(i)[0], 0, 0, 0))]
        out_specs.append(part)
        out_shape.append(jax.ShapeDtypeStruct(old.shape, old.dtype))
        window_cfg = (n_parts, n_seq, t_new)
    return pl.pallas_call(
        functools.partial(_premix_kernel, dils=tuple(dils), chunk=chunk, per_seq=per_seq, skip=skip,
                          window=window_cfg),
        grid=(m // tm,),
        in_specs=in_specs, out_specs=out_specs, out_shape=out_shape,
        compiler_params=_params(1),
        name="premix",
    )(*args)


def _attn_prompt_kernel(q_ref, kc_ref, kp_ref, vc_ref, vp_ref, bias_ref, o_ref, l_ref, kbuf, vbuf, tab_ref,
                        *, qb):
    nk = 2 * N_STEP

    @pl.when((pl.program_id(0) == 0) & (pl.program_id(1) == 0) & (pl.program_id(2) == 0))
    def _():
        no_prev = lax.broadcasted_iota(jnp.int32, (N_STEP, nk), 1) < N_STEP
        for h in range(N_HEADS):
            row = jnp.broadcast_to(bias_ref[h:h + 1, :], (N_STEP, nk))
            tab = pltpu.roll(row, 0, 1, stride=1, stride_axis=0)
            tab_ref[0, h] = tab
            tab_ref[1, h] = jnp.where(no_prev, NEG, tab)

    first = pl.program_id(2) == 0
    kbuf[0:N_STEP, :] = kp_ref[0, 0]
    kbuf[N_STEP:, :] = kc_ref[0, 0]
    vbuf[0:N_STEP, :] = vp_ref[0, 0]
    vbuf[N_STEP:, :] = vc_ref[0, 0]
    lo_half = lax.broadcasted_iota(jnp.int32, (N_STEP, LANE), 1) < HEAD_DIM
    keep_lo = jnp.where(lo_half, 1.0, 0.0).astype(BF16)
    keep_hi = jnp.where(lo_half, 0.0, 1.0).astype(BF16)

    def block(qi, carry):
        r0 = pl.multiple_of(qi * N_STEP, N_STEP)
        which = jnp.where(first & (qi == 0), 1, 0)
        for hp in range(N_HEADS // 2):
            lanes = slice(hp * LANE, (hp + 1) * LANE)
            qp = q_ref[0, 0, pl.ds(r0, N_STEP), lanes]
            kp = kbuf[pl.ds(r0, nk), lanes]
            vp = vbuf[pl.ds(r0, nk), lanes]
            o_pair, l_pair = None, None
            for hh in range(2):
                qm = qp * (keep_lo if hh == 0 else keep_hi)
                s = lax.dot_general(qm, kp, (((1,), (1,)), ((), ())), preferred_element_type=F32)
                s = s + tab_ref[which, 2 * hp + hh]
                m = jnp.max(s, axis=-1, keepdims=True)
                e = jnp.exp(s - m)
                den = jnp.sum(e, axis=-1, keepdims=True)
                o = _dot(e.astype(BF16), vp) / den
                lse = jnp.broadcast_to(m + jnp.log(den), (N_STEP, LANE))
                if hh == 0:
                    o_pair, l_pair = o, lse
                else:
                    o_pair = jnp.where(lo_half, o_pair, o)
                    l_pair = jnp.where(lo_half, l_pair, lse)
            o_ref[0, 0, pl.ds(r0, N_STEP), lanes] = o_pair
            l_ref[0, 0, pl.ds(r0, N_STEP), lanes] = l_pair
        return carry

    lax.fori_loop(0, qb // N_STEP, block, 0)


def _attn_prompt_branch(q, k, v, bias_row):
    batch, dil, sub, _ = q.shape
    qb = min(sub, 512)
    assert sub % qb == 0 and qb % N_STEP == 0
    per = qb // N_STEP
    cur = pl.BlockSpec((1, 1, qb, D_ATT), lambda b, r, i: (b, r, i, 0))
    prev = pl.BlockSpec((1, 1, N_STEP, D_ATT), lambda b, r, i: (b, r, jnp.maximum(i * per - 1, 0), 0))
    return pl.pallas_call(
        functools.partial(_attn_prompt_kernel, qb=qb),
        grid=(batch, dil, sub // qb),
        in_specs=[cur, cur, prev, cur, prev, _const_spec(bias_row.shape)],
        out_specs=[cur, cur],
        out_shape=[jax.ShapeDtypeStruct(q.shape, F32)] * 2,
        scratch_shapes=[pltpu.VMEM((qb + N_STEP, D_ATT), BF16)] * 2
                       + [pltpu.VMEM((2, N_HEADS, N_STEP, 2 * N_STEP), F32)],
        compiler_params=_params(3),
        name=f"attn_prompt_d{dil}",
    )(q, k, k, v, v, bias_row)


Q_PAD = 8


def _sample_steps(cache_len, t_new, n_new):
    step_of = np.full((len(BRANCHES), t_new, cache_len + n_new), -1, np.int32)
    for bi, (w, d) in enumerate(BRANCHES):
        for i in range(t_new):
            for j in range(w // d + 1):
                ext = cache_len + i - j * d
                assert ext >= 0
                step_of[bi, i, ext] = j
    return step_of


def _attn_sample_kernel(q_ref, kn_ref, vn_ref, kc_ref, vc_ref, tab_ref, o_ref, qs, *, t_new):
    n_new = kn_ref.shape[2]
    to_front = (n_new - pl.program_id(0) * t_new) % n_new
    q = q_ref[0].astype(F32)
    qs[...] = jnp.zeros(qs.shape, F32)
    for h in range(N_HEADS):
        qs[h, 0:t_new, :] = q[:, h * HEAD_DIM:(h + 1) * HEAD_DIM]
        qh = qs[h].astype(BF16)
        kn = pltpu.roll(kn_ref[h], to_front, 1).astype(BF16)
        vn = pltpu.roll(vn_ref[h], to_front, 1).astype(BF16)
        s = jnp.concatenate([_dot(qh, kc_ref[0, h].astype(BF16)), _dot(qh, kn)], axis=-1)
        logits = [s + tab_ref[bi, h] for bi in range(len(BRANCHES))]
        m = jnp.max(functools.reduce(jnp.maximum, logits), axis=-1, keepdims=True)
        e = sum(jnp.exp(l - m) for l in logits)
        den = jnp.sum(e, axis=-1, keepdims=True)
        eb = e.astype(BF16)
        nt = (((1,), (1,)), ((), ()))
        o = (lax.dot_general(eb[:, :-n_new], vc_ref[0, h].astype(BF16), nt, preferred_element_type=F32)
             + lax.dot_general(eb[:, -n_new:], vn, nt, preferred_element_type=F32)) / den
        o_ref[0, :, h * HEAD_DIM:(h + 1) * HEAD_DIM] = o[:t_new]


def _attn_sample(q, k_new, v_new, cache_k, cache_v, layer, tab, batch, t_new):
    cache_len = cache_k.shape[-1]
    tok = pl.BlockSpec((1, t_new, D_ATT), lambda b: (b, 0, 0))
    slab = pl.BlockSpec((pl.Squeezed(), 1, N_HEADS, HEAD_DIM, cache_len), lambda b: (layer, b, 0, 0, 0))
    out = pl.pallas_call(
        functools.partial(_attn_sample_kernel, t_new=t_new),
        grid=(batch,),
        in_specs=[tok, _const_spec(k_new.shape), _const_spec(v_new.shape), slab, slab, _const_spec(tab.shape)],
        out_specs=tok,
        out_shape=jax.ShapeDtypeStruct((batch, t_new, D_ATT), F32),
        scratch_shapes=[pltpu.VMEM((N_HEADS, Q_PAD, HEAD_DIM), F32)],
        compiler_params=_params(1),
        name="attn_sample",
    )(q.reshape(batch, t_new, D_ATT), k_new, v_new, cache_k, cache_v, tab)
    return out.reshape(batch * t_new, D_ATT)


def _ssm_kernel(u_ref, h0_ref, w_ref, pw_ref, qw_ref, a_ref, y_ref, hl_ref, z_buf, hb_buf,
                *, chunk, scan_rows):
    n_pair = chunk // TOK_PER_PAIR
    ub = u_ref[0]
    z = _dot(ub, pw_ref[0])
    ar = a_ref[0, 0:1, :]
    ai = a_ref[0, 1:2, :]
    h0 = h0_ref[0, 0]
    if scan_rows:
        z_buf[...] = z

        def step(r, h):
            hr, hi = h
            hb_buf[pl.ds(r, 1), 0:PLANE] = hr
            hb_buf[pl.ds(r, 1), PLANE:] = hi
            zr = z_buf[pl.ds(r, 1), 0:PLANE]
            zi = z_buf[pl.ds(r, 1), PLANE:]
            return ar * hr - ai * hi + zr, ar * hi + ai * hr + zi

        hr, hi = lax.fori_loop(0, ub.shape[0], step, (h0[:, 0:PLANE], h0[:, PLANE:]))
        hl_ref[0, 0, :, 0:PLANE] = hr
        hl_ref[0, 0, :, PLANE:] = hi
        hb = hb_buf[...].astype(BF16)
    else:
        hr, hi = h0[:, 0:PLANE], h0[:, PLANE:]
        hl_ref[0, 0, :, 0:PLANE] = ar * hr - ai * hi + z[:, 0:PLANE]
        hl_ref[0, 0, :, PLANE:] = ar * hi + ai * hr + z[:, PLANE:]
        hb = h0.astype(BF16)
    for b in range(n_pair):
        acc = _dot(hb, qw_ref[0, b])
        for a in range(b + 1):
            acc = acc + _dot(ub[:, a * MXU:(a + 1) * MXU], w_ref[0, b - a])
        y_ref[0, :, b * MXU:(b + 1) * MXU] = acc


def _ssm(uc, h0, tabs, chunk, rows_per_block, scan_rows):
    w, pw, qw, a_pow = tabs
    rows, width = uc.shape[1:]
    n_blocks = rows // rows_per_block
    seqs = h0.shape[2]
    return pl.pallas_call(
        functools.partial(_ssm_kernel, chunk=chunk, scan_rows=scan_rows),
        grid=(N_JT, n_blocks),
        in_specs=[pl.BlockSpec((1, rows_per_block, width), lambda j, b: (j, b, 0)),
                  pl.BlockSpec((1, 1, seqs, 2 * PLANE), lambda j, b: (j, b, 0, 0)),
                  pl.BlockSpec((1,) + w.shape[1:], lambda j, b: (j, 0, 0, 0)),
                  pl.BlockSpec((1,) + pw.shape[1:], lambda j, b: (j, 0, 0)),
                  pl.BlockSpec((1,) + qw.shape[1:], lambda j, b: (j, 0, 0, 0)),
                  pl.BlockSpec((1,) + a_pow.shape[1:], lambda j, b: (j, 0, 0))],
        out_specs=[pl.BlockSpec((1, rows_per_block, width), lambda j, b: (j, b, 0)),
                   pl.BlockSpec((1, 1, seqs, 2 * PLANE), lambda j, b: (j, b, 0, 0))],
        out_shape=[jax.ShapeDtypeStruct(uc.shape, F32),
                   jax.ShapeDtypeStruct(h0.shape, F32)],
        scratch_shapes=[pltpu.VMEM((rows_per_block, 2 * PLANE), F32)] * 2,
        compiler_params=_params(2),
        name=f"ssm_chunk{chunk}",
    )(uc, h0, w, pw, qw, a_pow)


def _ssm_tables(a_re, a_im, log_dt, b_re, b_im, c_re, c_im, chunk):
    hi = lax.Precision.HIGHEST
    dt = jnp.exp(log_dt)[:, None]
    den = a_re * a_re + a_im * a_im
    er, ang = jnp.exp(a_re * dt), a_im * dt
    ab_re, ab_im = er * jnp.cos(ang), er * jnp.sin(ang)
    nr, ni = ab_re - 1.0, ab_im
    f_re, f_im = (nr * a_re + ni * a_im) / den, (ni * a_re - nr * a_im) / den
    bb_re = f_re[..., None] * b_re - f_im[..., None] * b_im
    bb_im = f_re[..., None] * b_im + f_im[..., None] * b_re

    def power(taus):
        taus = jnp.asarray(taus, F32)[:, None, None]
        mag, ang_t = jnp.exp(a_re * dt * taus), a_im * dt * taus
        return mag * jnp.cos(ang_t), mag * jnp.sin(ang_t)

    p_re, p_im = power(np.arange(chunk + 1))
    pr, pi = (x[..., None] for x in power(chunk - 1 - np.arange(chunk)))
    s_re, s_im = pr * bb_re - pi * bb_im, pr * bb_im + pi * bb_re
    qr, qi = p_re[1:, :, None, :], p_im[1:, :, None, :]
    e_re, e_im = c_re * qr - c_im * qi, c_re * qi + c_im * qr
    lr, li = p_re[:chunk, :, None, :], p_im[:chunk, :, None, :]
    cl_re, cl_im = c_re * lr - c_im * li, c_re * li + c_im * lr
    kern = (jnp.einsum("tgon,gni->tgoi", cl_re, bb_re, precision=hi)
            - jnp.einsum("tgon,gni->tgoi", cl_im, bb_im, precision=hi))

    eye = jnp.eye(GROUPS_PER_TILE, dtype=F32)
    n_pair = chunk // TOK_PER_PAIR
    tile = lambda x: x.reshape((x.shape[0], N_JT, GROUPS_PER_TILE) + x.shape[2:])
    kern_t = tile(kern)
    kern_t = jnp.concatenate([kern_t, jnp.zeros_like(kern_t[:1])], axis=0)
    lag = (TOK_PER_PAIR * np.arange(n_pair)[:, None, None]
           + np.arange(TOK_PER_PAIR)[None, None, :] - np.arange(TOK_PER_PAIR)[None, :, None])
    kl = kern_t[lag]
    w = jnp.einsum("dpqjgoi,gh->jdpgiqho", kl, eye)
    w = w.reshape(N_JT, n_pair, MXU, MXU)
    gi = GROUPS_PER_TILE
    row_g = (np.arange(chunk * LANE) // SSM_GROUP) % gi
    mask_p = jnp.asarray(row_g[:, None] == (np.arange(PLANE) // SSM_STATE)[None], F32)

    def state_image(s):
        a = tile(s).transpose(1, 0, 2, 4, 3).reshape(N_JT, chunk * LANE, SSM_STATE)
        return jnp.tile(a, (1, 1, gi)) * mask_p

    pw = jnp.concatenate([state_image(s_re), state_image(s_im)], axis=-1)
    mask_q = jnp.asarray((np.arange(PLANE) // SSM_STATE)[:, None] == (np.arange(LANE) // SSM_GROUP)[None], F32)

    def readout(e):
        a = tile(e).transpose(1, 0, 2, 4, 3)
        a = a.reshape(N_JT, n_pair, TOK_PER_PAIR, gi, SSM_STATE, SSM_GROUP).transpose(0, 1, 3, 4, 2, 5)
        a = a.reshape(N_JT, n_pair, PLANE, TOK_PER_PAIR, SSM_GROUP)
        return jnp.concatenate([jnp.tile(a[:, :, :, t], (1, 1, 1, gi)) * mask_q
                                for t in range(TOK_PER_PAIR)], axis=-1)

    qw = jnp.concatenate([readout(e_re), readout(-e_im)], axis=2)
    a_pow = jnp.stack([p_re[chunk].reshape(N_JT, PLANE), p_im[chunk].reshape(N_JT, PLANE)], axis=1)
    return w.astype(BF16), pw.astype(BF16), qw.astype(BF16), a_pow


def _state_to_tiles(re, im, n_blocks):
    n_seq = re.shape[0]
    t = lambda x: x.reshape(n_blocks, n_seq // n_blocks, N_JT, PLANE).transpose(2, 0, 1, 3)
    return jnp.concatenate([t(re), t(im)], axis=-1)


def _tiles_to_state(h):
    n_seq = h.shape[1] * h.shape[2]
    t = lambda x: x.transpose(1, 2, 0, 3).reshape(n_seq, N_GROUPS, SSM_STATE)
    return t(h[..., :PLANE]), t(h[..., PLANE:])


def _postmix_kernel(*refs, dils, chunk, final):
    it = iter(refs)
    x1_ref = next(it)
    att_refs = [(next(it), next(it)) for _ in dils] if len(dils) > 1 else [(next(it), None)]
    y_ref, uf_ref, p_ref = next(it), next(it), next(it)
    (natt_ref, nssm_ref, dskip_ref, wglu_ref, bglu_ref, wout_ref, nf_ref, wg_ref, wu_ref, wd_ref,
     nple_ref, wpg_ref, wpp_ref) = (next(it) for _ in range(13))
    perm_t = {d: next(it) for d in sorted((set(dils) | {chunk}) - {1})}
    nfin_ref = next(it) if final else None
    out_ref, ybuf = next(it), next(it)
    tm = x1_ref.shape[0]

    def natural(ref, d):
        val = ref[...] if d == 1 else ref[0].reshape(tm, D_ATT)
        return val if d == 1 else _permute_f32(perm_t[d], val)

    if len(dils) == 1:
        att = att_refs[0][0][...]
    else:
        outs = [natural(o, d) for d, (o, _) in zip(dils, att_refs)]
        lses = [natural(l, d) for d, (_, l) in zip(dils, att_refs)]
        top = functools.reduce(jnp.maximum, lses)
        ws = [jnp.exp(l - top) for l in lses]
        att = sum(w * o for w, o in zip(ws, outs)) / sum(ws)
    a_n = _rms(att, natt_ref[...])
    rows = tm // chunk
    for t in range(chunk):
        for j in range(N_JT):
            ybuf[t * rows:(t + 1) * rows, j * LANE:(j + 1) * LANE] = y_ref[j, :, t * LANE:(t + 1) * LANE]
    y = _permute_f32(perm_t[chunk], ybuf[...]) + dskip_ref[...] * uf_ref[...]
    z = jax.nn.gelu(y)
    s = z * jax.nn.sigmoid(_dot(z.astype(BF16), wglu_ref[...]) + bglu_ref[...])
    s_n = _rms(s, nssm_ref[...])
    mixed = jnp.concatenate([a_n, s_n], axis=-1).astype(BF16)
    x2 = x1_ref[...] + _dot(mixed, wout_ref[...])
    x3 = _ffn(x2, nf_ref[...], wg_ref, wu_ref, wd_ref)
    gate = jax.nn.sigmoid(_dot(_rms(x3, nple_ref[...]).astype(BF16), wpg_ref[...]))
    x4 = x3 + gate * _dot(p_ref[...].astype(BF16), wpp_ref[...])
    out_ref[...] = _rms(x4, nfin_ref[...]) if final else x4


def _postmix(x1, att_planes, dils, y, uf, p, layer, weights, perms_t, nfin, tm, seq, chunk):
    m = x1.shape[0]
    per_seq = seq // tm
    final = nfin is not None
    row = lambda w: pl.BlockSpec((tm, w), lambda i: (i, 0))
    att_args, att_specs = [], []
    for d, (o, l) in zip(dils, att_planes):
        spec = row(D_ATT) if d == 1 else pl.BlockSpec((1, d, tm // d, D_ATT),
                                                       lambda i: (i // per_seq, 0, i % per_seq, 0))
        for a in ((o, l) if len(dils) > 1 else (o,)):
            att_args.append(a.reshape(m, D_ATT) if d == 1 else a)
            att_specs.append(spec)
    consts = list(weights) + [perms_t[d] for d in sorted((set(dils) | {chunk}) - {1})] + ([nfin] if final else [])
    args = [x1] + att_args + [y, uf, p] + consts
    in_specs = ([row(D_MODEL)] + att_specs
                + [pl.BlockSpec((N_JT, tm // chunk, chunk * LANE), lambda i: (0, i, 0)), row(D_SSM),
                   pl.BlockSpec((pl.Squeezed(), tm, D_PLE), lambda i: (layer, i, 0))]
                + [_const_spec(c.shape) for c in consts])
    return pl.pallas_call(
        functools.partial(_postmix_kernel, dils=tuple(dils), chunk=chunk, final=final),
        grid=(m // tm,),
        in_specs=in_specs,
        out_specs=row(D_MODEL),
        out_shape=jax.ShapeDtypeStruct((m, D_MODEL), F32),
        scratch_shapes=[pltpu.VMEM((tm, D_SSM), F32)],
        compiler_params=_params(1),
        name="postmix",
    )(*args)


def _t5_bucket(dist):
    dist = np.asarray(dist, dtype=np.int64)
    exact = N_BUCKETS // 2
    ratio = np.log(np.maximum(dist, 1) / exact) / np.log(MAX_DISTANCE / exact)
    large = np.minimum(exact + (ratio * (N_BUCKETS - exact)).astype(np.int64), N_BUCKETS - 1)
    return np.where(dist < exact, dist, large).astype(np.int32)


def _branch_bias(rel_bias, dil):
    return rel_bias[_t5_bucket(np.arange(N_STEP + 1) * dil)].T.astype(F32)


def _prompt_bias_row(rel_bias, dil):
    step = N_STEP - np.arange(2 * N_STEP)
    row = rel_bias[_t5_bucket(np.maximum(step, 0) * dil)].T.astype(F32)
    return jnp.where((step >= 0)[None], row, NEG)


def _sample_table(biases, step_of):
    n_br, t_new, cols = step_of.shape
    tabs = []
    for bi in range(n_br):
        vals = biases[bi][:, np.maximum(step_of[bi], 0)]
        tab = jnp.where((step_of[bi] >= 0)[None], vals, NEG)
        tabs.append(jnp.concatenate([tab, jnp.zeros((N_HEADS, Q_PAD - t_new, cols), F32)], axis=1))
    return jnp.stack(tabs)


def kernel(x_prompt, x_sample, p_prompt, p_sample, cache_k, cache_v, state_ssm_re, state_ssm_im,
           rel_bias, w_in, w_out, norm_mix, norm_att_out, norm_ssm_out, norm_ffn, ffn_w_gate,
           ffn_w_up, ffn_w_down, ssm_a_re, ssm_a_im, ssm_log_dt, ssm_b_re, ssm_b_im, ssm_c_re,
           ssm_c_im, ssm_d, w_glu, b_glu, norm_ple, w_ple_gate, w_ple_proj, norm_final):
    depth = w_in.shape[0]
    batch, seq, _ = x_prompt.shape
    dec_batch, dec_seq, _ = x_sample.shape
    cache_len = cache_k.shape[2]
    keep = min(BRANCHES[-1][0], seq)
    dils = tuple(d for _, d in BRANCHES)
    chunk_p = dils[-1]
    tm_p = 256
    tm_s = dec_batch * dec_seq
    assert seq % tm_p == 0 and chunk_p in dils and dec_seq % TOK_PER_PAIR == 0 and tm_s % 8 == 0

    xp = x_prompt.reshape(batch * seq, D_MODEL)
    xs = x_sample.reshape(tm_s, D_MODEL)
    biases = [_branch_bias(rel_bias, d) for d in dils]
    tabs_p = [_prompt_bias_row(rel_bias, d) for d in dils]
    tab_s = _sample_table(biases, _sample_steps(cache_len, dec_seq, tm_s))
    to_lanes = lambda a: a.transpose(0, 1, 3, 4, 2)
    from_lanes = lambda a: a.transpose(0, 1, 4, 2, 3)
    cache_kt, cache_vt = to_lanes(cache_k), to_lanes(cache_v)
    perms_p = {d: jnp.asarray(_perm_matrix(tm_p, d), BF16) for d in dils if d > 1}
    perms_pt = {d: jnp.asarray(_perm_matrix(tm_p, d).T, BF16) for d in dils if d > 1}
    perms_s = {dec_seq: jnp.asarray(_perm_matrix(tm_s, dec_seq), BF16)}
    perms_st = {dec_seq: jnp.asarray(_perm_matrix(tm_s, dec_seq).T, BF16)}
    row = lambda v: v.reshape(1, -1)
    bf = lambda a: a.astype(BF16)

    def layer_weights(i):
        pre = (row(norm_ffn[i, 0]), bf(ffn_w_gate[i, 0]), bf(ffn_w_up[i, 0]), bf(ffn_w_down[i, 0]),
               row(norm_mix[i]), bf(w_in[i]))
        post = (row(norm_att_out[i]), row(norm_ssm_out[i]), row(ssm_d[i]), bf(w_glu[i]), row(b_glu[i]),
                bf(w_out[i]), row(norm_ffn[i, 1]), bf(ffn_w_gate[i, 1]), bf(ffn_w_up[i, 1]),
                bf(ffn_w_down[i, 1]), row(norm_ple[i]), bf(w_ple_gate[i]), bf(w_ple_proj[i]))
        nfin = row(norm_final) if i == depth - 1 else None
        ssm_par = (ssm_a_re[i], ssm_a_im[i], ssm_log_dt[i], ssm_b_re[i], ssm_b_im[i], ssm_c_re[i], ssm_c_im[i])
        return pre, post, nfin, ssm_par

    k_news, v_news, rss, iss = [], [], [], []
    for i in range(depth):
        pre, post, nfin, ssm_par = layer_weights(i)
        x1, q, _, _, kf, vf, uf, uc = _premix(xs, pre, perms_s, tm_s, tm_s, tm_s, (), dec_seq)
        att = _attn_sample(q, kf[0], vf[0], cache_kt, cache_vt, i, tab_s, dec_batch, dec_seq)
        h0 = _state_to_tiles(state_ssm_re[i], state_ssm_im[i], 1)
        y, h_last = _ssm(uc, h0, _ssm_tables(*ssm_par, dec_seq), dec_seq, dec_batch, False)
        xs = _postmix(x1, [(att, None)], (1,), y, uf, p_sample.reshape(depth, tm_s, D_PLE), i, post, perms_st,
                      nfin, tm_s, tm_s, dec_seq)
        k_news.append(kf[0])
        v_news.append(vf[0])
        re, im = _tiles_to_state(h_last)
        rss.append(re)
        iss.append(im)

    assert depth >= 2
    windows = [(cache_kt, jnp.stack(k_news), dec_seq), (cache_vt, jnp.stack(v_news), dec_seq)]
    kp, vp, rp, ip = [], [], [], []
    for i in range(depth):
        pre, post, nfin, ssm_par = layer_weights(i)
        outs = list(_premix(xp, pre, perms_p, tm_p, seq, keep, dils[1:], chunk_p,
                            window=windows[i] if i < 2 else None))
        if i < 2:
            windows[i] = outs.pop()
        x1, q, k, v = outs[:4]
        kf, vf, uf, uc = outs[-4:]
        nat = lambda a: a.reshape(batch, 1, seq, D_ATT)
        qkv = [(nat(q), nat(k), nat(v))] + [tuple(outs[4 + 3 * n:7 + 3 * n]) for n in range(len(dils) - 1)]
        planes = [_attn_prompt_branch(*qkv[bi], tabs_p[bi]) for bi in range(len(dils))]
        zeros = jnp.zeros((batch, N_GROUPS, SSM_STATE), F32)
        y, h_last = _ssm(uc, _state_to_tiles(zeros, zeros, batch), _ssm_tables(*ssm_par, chunk_p),
                         chunk_p, seq // chunk_p, True)
        xp = _postmix(x1, planes, dils, y, uf, p_prompt.reshape(depth, batch * seq, D_PLE), i, post, perms_pt,
                      nfin, tm_p, seq, chunk_p)
        kp.append(kf)
        vp.append(vf)
        re, im = _tiles_to_state(h_last)
        rp.append(re)
        ip.append(im)

    return (xp.reshape(batch, seq, D_MODEL), xs.reshape(dec_batch, dec_seq, D_MODEL),
            from_lanes(jnp.stack(kp)), from_lanes(jnp.stack(vp)), jnp.stack(rp), jnp.stack(ip),
            from_lanes(windows[0]), from_lanes(windows[1]), jnp.stack(rss), jnp.stack(iss))
```

```python
import functools
import math

import numpy as np
import jax
import jax.numpy as jnp
from jax import lax
from jax.experimental import pallas as pl
from jax.experimental.pallas import tpu as pltpu

F32 = jnp.float32
BF16 = jnp.bfloat16

D_MODEL = 1024
D_ATT = 512
D_SSM = 512
HEAD_DIM = 64
N_HEADS = 8
SSM_GROUP = 16
N_GROUPS = 32
SSM_STATE = 64
BRANCHES = ((128, 1), (512, 4), (2048, 16))
N_STEP = 128
N_BUCKETS = 32
MAX_DISTANCE = 2048
D_FF = 2816
D_PLE = 256
D_IN = 3 * D_ATT + D_SSM
EPS = 1e-6
NEG = -1e30

LANE = 128
SUB = 8
MXU = 256
GROUPS_PER_TILE = LANE // SSM_GROUP
N_JT = D_SSM // LANE
PLANE = GROUPS_PER_TILE * SSM_STATE
TOK_PER_PAIR = MXU // LANE
VMEM_LIMIT = 56 * 1024 * 1024

assert all(w // d == N_STEP for w, d in BRANCHES)


def _rms(x, g):
    return x * lax.rsqrt(jnp.mean(x * x, axis=-1, keepdims=True) + EPS) * g


def _dot(a, b):
    return jnp.dot(a, b, preferred_element_type=F32)


def _ffn(x, g, wg_ref, wu_ref, wd_ref):
    h = _rms(x, g).astype(BF16)
    act = (jax.nn.silu(_dot(h, wg_ref[...])) * _dot(h, wu_ref[...])).astype(BF16)
    return x + 0.5 * _dot(act, wd_ref[...])


def _perm_matrix(n, d):
    p = np.zeros((n, n), np.float32)
    s, r = np.meshgrid(np.arange(n // d), np.arange(d), indexing="ij")
    p[(r * (n // d) + s).ravel(), (d * s + r).ravel()] = 1.0
    return p


def _permute_f32(pt_ref, x):
    hi = x.astype(BF16)
    lo = (x - hi.astype(F32)).astype(BF16)
    pt = pt_ref[...]
    return _dot(pt, hi) + _dot(pt, lo)


def _const_spec(shape):
    nd = len(shape)
    return pl.BlockSpec(shape, lambda *_: (0,) * nd, pipeline_mode=pl.Buffered(1))


def _params(n_axes):
    return pltpu.CompilerParams(dimension_semantics=("arbitrary",) * n_axes,
                                vmem_limit_bytes=VMEM_LIMIT)


def _window_part(old_ref, next_ref, new_ref, out_ref, part, n_parts, seq_idx, t_new):
    width, n_new = old_ref.shape[-1], new_ref.shape[-1]
    to_back = (LANE - t_new - seq_idx * t_new) % n_new
    tail = lax.broadcasted_iota(jnp.int32, (HEAD_DIM, LANE), 1) >= LANE - t_new
    for h in range(N_HEADS):
        shifted = pltpu.roll(old_ref[0, 0, h], width - t_new, 1)
        incoming = jnp.where(part == n_parts - 1,
                             pltpu.roll(new_ref[0, h], to_back, 1)[:, :LANE],
                             pltpu.roll(next_ref[0, 0, h], LANE - t_new, 1))
        out_ref[0, 0, h, :, :width - LANE] = shifted[:, :width - LANE]
        out_ref[0, 0, h, :, width - LANE:] = jnp.where(tail, incoming, shifted[:, width - LANE:])


def _premix_kernel(*refs, dils, chunk, per_seq, skip, window):
    it = iter(refs)
    x_ref, nf_ref, wg_ref, wu_ref, wd_ref, nm_ref, win_ref = (next(it) for _ in range(7))
    perm = {d: next(it) for d in sorted(set(dils) | {chunk})}
    window_in = [next(it) for _ in range(3)] if window else None
    x1_ref, q_ref, k_ref, v_ref = (next(it) for _ in range(4))
    strided = {d: (next(it), next(it), next(it)) for d in dils}
    kf_ref, vf_ref, uf_ref, uc_ref = (next(it) for _ in range(4))
    if window:
        n_parts, n_seq, t_new = window
        step = pl.program_id(0)
        _window_part(*window_in, next(it), step % n_parts, n_parts, (step // n_parts) % n_seq, t_new)

    x1 = _ffn(x_ref[...], nf_ref[...], wg_ref, wu_ref, wd_ref)
    x1_ref[...] = x1
    h = _rms(x1, nm_ref[...]).astype(BF16)
    qkvu = _dot(h, win_ref[...])
    q = (qkvu[:, :D_ATT] * (1.0 / math.sqrt(HEAD_DIM))).astype(BF16)
    kf = qkvu[:, D_ATT:2 * D_ATT]
    vf = qkvu[:, 2 * D_ATT:3 * D_ATT]
    uf = qkvu[:, 3 * D_ATT:]
    k, v, u = kf.astype(BF16), vf.astype(BF16), uf.astype(BF16)
    q_ref[...], k_ref[...], v_ref[...] = q, k, v
    uf_ref[...] = uf
    tm = x1.shape[0]

    @pl.when(pl.program_id(0) % per_seq >= skip)
    def _():
        kf_ref[0] = kf.T.reshape(N_HEADS, HEAD_DIM, tm)
        vf_ref[0] = vf.T.reshape(N_HEADS, HEAD_DIM, tm)

    for d, outs in strided.items():
        for val, out in zip((q, k, v), outs):
            out[0] = _dot(perm[d][...], val).astype(BF16).reshape(d, tm // d, D_ATT)
    up = _dot(perm[chunk][...], u).astype(BF16)
    rows = tm // chunk
    for t in range(chunk):
        for j in range(N_JT):
            uc_ref[j, :, t * LANE:(t + 1) * LANE] = up[t * rows:(t + 1) * rows, j * LANE:(j + 1) * LANE]


def _premix(x, weights, perms, tm, seq, keep, dils, chunk, window=None):
    m = x.shape[0]
    per_seq = seq // tm
    skip = (seq - keep) // tm
    assert keep % tm == 0 and tm % LANE == 0
    kept = pl.BlockSpec((1, N_HEADS, HEAD_DIM, tm),
                        lambda i: (i // per_seq, 0, 0, jnp.maximum(i % per_seq - skip, 0)))
    kept_shape = jax.ShapeDtypeStruct((m // seq, N_HEADS, HEAD_DIM, keep), F32)
    row = lambda w: pl.BlockSpec((tm, w), lambda i: (i, 0))
    consts = list(weights) + [perms[d] for d in sorted(set(dils) | {chunk})]
    in_specs = [row(D_MODEL)] + [_const_spec(c.shape) for c in consts]
    out_specs = [row(D_MODEL), row(D_ATT), row(D_ATT), row(D_ATT)]
    out_shape = [jax.ShapeDtypeStruct((m, D_MODEL), F32)] + [jax.ShapeDtypeStruct((m, D_ATT), BF16)] * 3
    for d in dils:
        out_specs += [pl.BlockSpec((1, d, tm // d, D_ATT), lambda i: (i // per_seq, 0, i % per_seq, 0))] * 3
        out_shape += [jax.ShapeDtypeStruct((m // seq, d, seq // d, D_ATT), BF16)] * 3
    out_specs += [kept, kept, row(D_SSM), pl.BlockSpec((N_JT, tm // chunk, chunk * LANE), lambda i: (0, i, 0))]
    out_shape += [kept_shape, kept_shape, jax.ShapeDtypeStruct((m, D_SSM), F32),
                  jax.ShapeDtypeStruct((N_JT, m // chunk, chunk * LANE), BF16)]
    args, window_cfg = [x] + consts, None
    if window is not None:
        old, new, t_new = window
        depth, n_seq, length = old.shape[0], old.shape[1], old.shape[-1]
        n_parts = (m // tm) // (depth * n_seq)
        width = length // n_parts
        assert n_parts * depth * n_seq == m // tm and width * n_parts == length and width % LANE == 0
        assert new.shape[-1] == n_seq * t_new and new.shape[-1] % LANE == 0 and t_new <= LANE
        tiles = width // LANE
        where = lambda i: (i // (n_parts * n_seq), (i // n_parts) % n_seq, i % n_parts)
        part = pl.BlockSpec((1, 1, N_HEADS, HEAD_DIM, width), lambda i: (*where(i)[:2], 0, 0, where(i)[2]))
        after = pl.BlockSpec((1, 1, N_HEADS, HEAD_DIM, LANE),
                             lambda i: (*where(i)[:2], 0, 0,
                                        jnp.minimum((where(i)[2] + 1) * tiles, length // LANE - 1)))
        args += [old, old, new]
        in_specs += [part, after, pl.BlockSpec((1,) + new.shape[1:], lambda i: (where(i)[0], 0, 0, 0))]
        out_specs.append(part)
        out_shape.append(jax.ShapeDtypeStruct(old.shape, old.dtype))
        window_cfg = (n_parts, n_seq, t_new)
    return pl.pallas_call(
        functools.partial(_premix_kernel, dils=tuple(dils), chunk=chunk, per_seq=per_seq, skip=skip,
                          window=window_cfg),
        grid=(m // tm,),
        in_specs=in_specs, out_specs=out_specs, out_shape=out_shape,
        compiler_params=_params(1),
        name="premix",
    )(*args)


def _attn_prompt_kernel(q_ref, kc_ref, kp_ref, vc_ref, vp_ref, bias_ref, o_ref, l_ref, kbuf, vbuf, tab_ref,
                        *, qb):
    nk = 2 * N_STEP

    @pl.when((pl.program_id(0) == 0) & (pl.program_id(1) == 0) & (pl.program_id(2) == 0))
    def _():
        no_prev = lax.broadcasted_iota(jnp.int32, (N_STEP, nk), 1) < N_STEP
        for h in range(N_HEADS):
            row = jnp.broadcast_to(bias_ref[h:h + 1, :], (N_STEP, nk))
            tab = pltpu.roll(row, 0, 1, stride=1, stride_axis=0)
            tab_ref[0, h] = tab
            tab_ref[1, h] = jnp.where(no_prev, NEG, tab)

    first = pl.program_id(2) == 0
    kbuf[0:N_STEP, :] = kp_ref[...]
    kbuf[N_STEP:, :] = kc_ref[...]
    vbuf[0:N_STEP, :] = vp_ref[...]
    vbuf[N_STEP:, :] = vc_ref[...]
    lo_half = lax.broadcasted_iota(jnp.int32, (N_STEP, LANE), 1) < HEAD_DIM
    keep_lo = jnp.where(lo_half, 1.0, 0.0).astype(BF16)
    keep_hi = jnp.where(lo_half, 0.0, 1.0).astype(BF16)

    def block(qi, carry):
        r0 = pl.multiple_of(qi * N_STEP, N_STEP)
        which = jnp.where(first & (qi == 0), 1, 0)
        for hp in range(N_HEADS // 2):
            lanes = slice(hp * LANE, (hp + 1) * LANE)
            qp = q_ref[pl.ds(r0, N_STEP), lanes]
            kp = kbuf[pl.ds(r0, nk), lanes]
            vp = vbuf[pl.ds(r0, nk), lanes]
            o_pair, l_pair = None, None
            for hh in range(2):
                qm = qp * (keep_lo if hh == 0 else keep_hi)
                s = lax.dot_general(qm, kp, (((1,), (1,)), ((), ())), preferred_element_type=F32)
                s = s + tab_ref[which, 2 * hp + hh]
                m = jnp.max(s, axis=-1, keepdims=True)
                e = jnp.exp(s - m)
                den = jnp.sum(e, axis=-1, keepdims=True)
                o = _dot(e.astype(BF16), vp) / den
                lse = jnp.broadcast_to(m + jnp.log(den), (N_STEP, LANE))
                if hh == 0:
                    o_pair, l_pair = o, lse
                else:
                    o_pair = jnp.where(lo_half, o_pair, o)
                    l_pair = jnp.where(lo_half, l_pair, lse)
            o_ref[pl.ds(r0, N_STEP), lanes] = o_pair
            l_ref[pl.ds(r0, N_STEP), lanes] = l_pair
        return carry

    lax.fori_loop(0, qb // N_STEP, block, 0)


def _attn_prompt_branch(q, k, v, bias_row):
    batch, dil, sub, _ = q.shape
    qb = min(sub, 512)
    assert sub % qb == 0 and qb % N_STEP == 0
    per = qb // N_STEP
    sq = pl.Squeezed()
    cur = pl.BlockSpec((sq, sq, qb, D_ATT), lambda b, r, i: (b, r, i, 0))
    prev = pl.BlockSpec((sq, sq, N_STEP, D_ATT), lambda b, r, i: (b, r, jnp.maximum(i * per - 1, 0), 0))
    return pl.pallas_call(
        functools.partial(_attn_prompt_kernel, qb=qb),
        grid=(batch, dil, sub // qb),
        in_specs=[cur, cur, prev, cur, prev, _const_spec(bias_row.shape)],
        out_specs=[cur, cur],
        out_shape=[jax.ShapeDtypeStruct(q.shape, F32)] * 2,
        scratch_shapes=[pltpu.VMEM((qb + N_STEP, D_ATT), BF16)] * 2
                       + [pltpu.VMEM((2, N_HEADS, N_STEP, 2 * N_STEP), F32)],
        compiler_params=_params(3),
        name=f"attn_prompt_d{dil}",
    )(q, k, k, v, v, bias_row)


Q_PAD = 8


def _sample_steps(cache_len, t_new, n_new):
    step_of = np.full((len(BRANCHES), t_new, cache_len + n_new), -1, np.int32)
    for bi, (w, d) in enumerate(BRANCHES):
        for i in range(t_new):
            for j in range(w // d + 1):
                ext = cache_len + i - j * d
                assert ext >= 0
                step_of[bi, i, ext] = j
    return step_of


def _attn_sample_kernel(q_ref, kn_ref, vn_ref, kc_ref, vc_ref, tab_ref, o_ref, qs, *, t_new):
    n_new = kn_ref.shape[2]
    to_front = (n_new - pl.program_id(0) * t_new) % n_new
    q = q_ref[0].astype(F32)
    qs[...] = jnp.zeros(qs.shape, F32)
    for h in range(N_HEADS):
        qs[h, 0:t_new, :] = q[:, h * HEAD_DIM:(h + 1) * HEAD_DIM]
        qh = qs[h].astype(BF16)
        kn = pltpu.roll(kn_ref[h], to_front, 1).astype(BF16)
        vn = pltpu.roll(vn_ref[h], to_front, 1).astype(BF16)
        s = jnp.concatenate([_dot(qh, kc_ref[0, h].astype(BF16)), _dot(qh, kn)], axis=-1)
        logits = [s + tab_ref[bi, h] for bi in range(len(BRANCHES))]
        m = jnp.max(functools.reduce(jnp.maximum, logits), axis=-1, keepdims=True)
        e = sum(jnp.exp(l - m) for l in logits)
        den = jnp.sum(e, axis=-1, keepdims=True)
        eb = e.astype(BF16)
        nt = (((1,), (1,)), ((), ()))
        o = (lax.dot_general(eb[:, :-n_new], vc_ref[0, h].astype(BF16), nt, preferred_element_type=F32)
             + lax.dot_general(eb[:, -n_new:], vn, nt, preferred_element_type=F32)) / den
        o_ref[0, :, h * HEAD_DIM:(h + 1) * HEAD_DIM] = o[:t_new]


def _attn_sample(q, k_new, v_new, cache_k, cache_v, layer, tab, batch, t_new):
    cache_len = cache_k.shape[-1]
    tok = pl.BlockSpec((1, t_new, D_ATT), lambda b: (b, 0, 0))
    slab = pl.BlockSpec((pl.Squeezed(), 1, N_HEADS, HEAD_DIM, cache_len), lambda b: (layer, b, 0, 0, 0))
    out = pl.pallas_call(
        functools.partial(_attn_sample_kernel, t_new=t_new),
        grid=(batch,),
        in_specs=[tok, _const_spec(k_new.shape), _const_spec(v_new.shape), slab, slab, _const_spec(tab.shape)],
        out_specs=tok,
        out_shape=jax.ShapeDtypeStruct((batch, t_new, D_ATT), F32),
        scratch_shapes=[pltpu.VMEM((N_HEADS, Q_PAD, HEAD_DIM), F32)],
        compiler_params=_params(1),
        name="attn_sample",
    )(q.reshape(batch, t_new, D_ATT), k_new, v_new, cache_k, cache_v, tab)
    return out.reshape(batch * t_new, D_ATT)


def _ssm_kernel(u_ref, h0_ref, w_ref, pw_ref, qw_ref, a_ref, y_ref, hl_ref, z_buf, hb_buf,
                *, chunk, scan_rows):
    n_pair = chunk // TOK_PER_PAIR
    ub = u_ref[0]
    z = _dot(ub, pw_ref[0])
    gr, gi = a_ref[0, 0], a_ref[0, 1]
    h0 = h0_ref[0, 0]

    def times(m, xr, xi):
        cr, ci = gr[m:m + 1], gi[m:m + 1]
        return cr * xr - ci * xi, cr * xi + ci * xr

    if scan_rows:
        rows = ub.shape[0]
        in_group = lax.broadcasted_iota(jnp.int32, (rows, PLANE), 0) % SUB

        def below(x, sh):
            return jnp.where(in_group >= sh, pltpu.roll(x, sh, 0), 0.0)

        lr, li = z[:, 0:PLANE], z[:, PLANE:]
        sh = 1
        while sh < SUB:
            pr, pi = times(sh, below(lr, sh), below(li, sh))
            lr, li = lr + pr, li + pi
            sh *= 2
        z_buf[:, 0:PLANE] = lr
        z_buf[:, PLANE:] = li

        def group(b, h):
            hr, hi = h
            r0 = pl.multiple_of(b * SUB, SUB)
            hb_buf[pl.ds(r0, SUB), 0:PLANE] = jnp.broadcast_to(hr, (SUB, PLANE))
            hb_buf[pl.ds(r0, SUB), PLANE:] = jnp.broadcast_to(hi, (SUB, PLANE))
            pr, pi = times(SUB, hr, hi)
            return (pr + z_buf[pl.ds(r0 + SUB - 1, 1), 0:PLANE], pi + z_buf[pl.ds(r0 + SUB - 1, 1), PLANE:])

        hr, hi = lax.fori_loop(0, rows // SUB, group, (h0[:, 0:PLANE], h0[:, PLANE:]))
        hl_ref[0, 0, :, 0:PLANE] = hr
        hl_ref[0, 0, :, PLANE:] = hi
        pw_r = jnp.broadcast_to(gr[0:SUB], (rows // SUB, SUB, PLANE)).reshape(rows, PLANE)
        pw_i = jnp.broadcast_to(gi[0:SUB], (rows // SUB, SUB, PLANE)).reshape(rows, PLANE)
        sr, si = hb_buf[:, 0:PLANE], hb_buf[:, PLANE:]
        hb = jnp.concatenate([pw_r * sr - pw_i * si + below(lr, 1), pw_r * si + pw_i * sr + below(li, 1)],
                             axis=-1).astype(BF16)
    else:
        hr, hi = h0[:, 0:PLANE], h0[:, PLANE:]
        pr, pi = times(1, hr, hi)
        hl_ref[0, 0, :, 0:PLANE] = pr + z[:, 0:PLANE]
        hl_ref[0, 0, :, PLANE:] = pi + z[:, PLANE:]
        hb = h0.astype(BF16)
    for b in range(n_pair):
        acc = _dot(hb, qw_ref[0, b])
        for a in range(b + 1):
            acc = acc + _dot(ub[:, a * MXU:(a + 1) * MXU], w_ref[0, b - a])
        y_ref[0, :, b * MXU:(b + 1) * MXU] = acc


def _ssm(uc, h0, tabs, chunk, rows_per_block, scan_rows):
    w, pw, qw, a_pow = tabs
    rows, width = uc.shape[1:]
    n_blocks = rows // rows_per_block
    seqs = h0.shape[2]
    return pl.pallas_call(
        functools.partial(_ssm_kernel, chunk=chunk, scan_rows=scan_rows),
        grid=(N_JT, n_blocks),
        in_specs=[pl.BlockSpec((1, rows_per_block, width), lambda j, b: (j, b, 0)),
                  pl.BlockSpec((1, 1, seqs, 2 * PLANE), lambda j, b: (j, b, 0, 0)),
                  pl.BlockSpec((1,) + w.shape[1:], lambda j, b: (j, 0, 0, 0)),
                  pl.BlockSpec((1,) + pw.shape[1:], lambda j, b: (j, 0, 0)),
                  pl.BlockSpec((1,) + qw.shape[1:], lambda j, b: (j, 0, 0, 0)),
                  pl.BlockSpec((1,) + a_pow.shape[1:], lambda j, b: (j, 0, 0, 0))],
        out_specs=[pl.BlockSpec((1, rows_per_block, width), lambda j, b: (j, b, 0)),
                   pl.BlockSpec((1, 1, seqs, 2 * PLANE), lambda j, b: (j, b, 0, 0))],
        out_shape=[jax.ShapeDtypeStruct(uc.shape, F32),
                   jax.ShapeDtypeStruct(h0.shape, F32)],
        scratch_shapes=[pltpu.VMEM((rows_per_block, 2 * PLANE), F32)] * 2,
        compiler_params=_params(2),
        name=f"ssm_chunk{chunk}",
    )(uc, h0, w, pw, qw, a_pow)


def _ssm_tables(a_re, a_im, log_dt, b_re, b_im, c_re, c_im, chunk):
    hi = lax.Precision.HIGHEST
    dt = jnp.exp(log_dt)[:, None]
    den = a_re * a_re + a_im * a_im
    er, ang = jnp.exp(a_re * dt), a_im * dt
    ab_re, ab_im = er * jnp.cos(ang), er * jnp.sin(ang)
    nr, ni = ab_re - 1.0, ab_im
    f_re, f_im = (nr * a_re + ni * a_im) / den, (ni * a_re - nr * a_im) / den
    bb_re = f_re[..., None] * b_re - f_im[..., None] * b_im
    bb_im = f_re[..., None] * b_im + f_im[..., None] * b_re

    def power(taus):
        taus = jnp.asarray(taus, F32)[:, None, None]
        mag, ang_t = jnp.exp(a_re * dt * taus), a_im * dt * taus
        return mag * jnp.cos(ang_t), mag * jnp.sin(ang_t)

    p_re, p_im = power(np.arange(chunk + 1))
    pr, pi = (x[..., None] for x in power(chunk - 1 - np.arange(chunk)))
    s_re, s_im = pr * bb_re - pi * bb_im, pr * bb_im + pi * bb_re
    qr, qi = p_re[1:, :, None, :], p_im[1:, :, None, :]
    e_re, e_im = c_re * qr - c_im * qi, c_re * qi + c_im * qr
    lr, li = p_re[:chunk, :, None, :], p_im[:chunk, :, None, :]
    cl_re, cl_im = c_re * lr - c_im * li, c_re * li + c_im * lr
    kern = (jnp.einsum("tgon,gni->tgoi", cl_re, bb_re, precision=hi)
            - jnp.einsum("tgon,gni->tgoi", cl_im, bb_im, precision=hi))

    eye = jnp.eye(GROUPS_PER_TILE, dtype=F32)
    n_pair = chunk // TOK_PER_PAIR
    tile = lambda x: x.reshape((x.shape[0], N_JT, GROUPS_PER_TILE) + x.shape[2:])
    kern_t = tile(kern)
    kern_t = jnp.concatenate([kern_t, jnp.zeros_like(kern_t[:1])], axis=0)
    lag = (TOK_PER_PAIR * np.arange(n_pair)[:, None, None]
           + np.arange(TOK_PER_PAIR)[None, None, :] - np.arange(TOK_PER_PAIR)[None, :, None])
    kl = kern_t[lag]
    w = jnp.einsum("dpqjgoi,gh->jdpgiqho", kl, eye)
    w = w.reshape(N_JT, n_pair, MXU, MXU)
    gi = GROUPS_PER_TILE
    row_g = (np.arange(chunk * LANE) // SSM_GROUP) % gi
    mask_p = jnp.asarray(row_g[:, None] == (np.arange(PLANE) // SSM_STATE)[None], F32)

    def state_image(s):
        a = tile(s).transpose(1, 0, 2, 4, 3).reshape(N_JT, chunk * LANE, SSM_STATE)
        return jnp.tile(a, (1, 1, gi)) * mask_p

    pw = jnp.concatenate([state_image(s_re), state_image(s_im)], axis=-1)
    mask_q = jnp.asarray((np.arange(PLANE) // SSM_STATE)[:, None] == (np.arange(LANE) // SSM_GROUP)[None], F32)

    def readout(e):
        a = tile(e).transpose(1, 0, 2, 4, 3)
        a = a.reshape(N_JT, n_pair, TOK_PER_PAIR, gi, SSM_STATE, SSM_GROUP).transpose(0, 1, 3, 4, 2, 5)
        a = a.reshape(N_JT, n_pair, PLANE, TOK_PER_PAIR, SSM_GROUP)
        return jnp.concatenate([jnp.tile(a[:, :, :, t], (1, 1, 1, gi)) * mask_q
                                for t in range(TOK_PER_PAIR)], axis=-1)

    qw = jnp.concatenate([readout(e_re), readout(-e_im)], axis=2)
    a_pow = jnp.stack([x.reshape(SUB + 1, N_JT, PLANE).transpose(1, 0, 2)
                       for x in power(chunk * np.arange(SUB + 1))], axis=1)
    return w.astype(BF16), pw.astype(BF16), qw.astype(BF16), a_pow


def _state_to_tiles(re, im, n_blocks):
    n_seq = re.shape[0]
    t = lambda x: x.reshape(n_blocks, n_seq // n_blocks, N_JT, PLANE).transpose(2, 0, 1, 3)
    return jnp.concatenate([t(re), t(im)], axis=-1)


def _tiles_to_state(h):
    n_seq = h.shape[1] * h.shape[2]
    t = lambda x: x.transpose(1, 2, 0, 3).reshape(n_seq, N_GROUPS, SSM_STATE)
    return t(h[..., :PLANE]), t(h[..., PLANE:])


def _postmix_kernel(*refs, dils, chunk, final):
    it = iter(refs)
    x1_ref = next(it)
    att_refs = [(next(it), next(it)) for _ in dils] if len(dils) > 1 else [(next(it), None)]
    y_ref, uf_ref, p_ref = next(it), next(it), next(it)
    (natt_ref, nssm_ref, dskip_ref, wglu_ref, bglu_ref, wout_ref, nf_ref, wg_ref, wu_ref, wd_ref,
     nple_ref, wpg_ref, wpp_ref) = (next(it) for _ in range(13))
    perm_t = {d: next(it) for d in sorted((set(dils) | {chunk}) - {1})}
    nfin_ref = next(it) if final else None
    out_ref, ybuf = next(it), next(it)
    tm = x1_ref.shape[0]

    def natural(ref, d):
        val = ref[...] if d == 1 else ref[0].reshape(tm, D_ATT)
        return val if d == 1 else _permute_f32(perm_t[d], val)

    if len(dils) == 1:
        att = att_refs[0][0][...]
    else:
        outs = [natural(o, d) for d, (o, _) in zip(dils, att_refs)]
        lses = [natural(l, d) for d, (_, l) in zip(dils, att_refs)]
        top = functools.reduce(jnp.maximum, lses)
        ws = [jnp.exp(l - top) for l in lses]
        att = sum(w * o for w, o in zip(ws, outs)) / sum(ws)
    a_n = _rms(att, natt_ref[...])
    rows = tm // chunk
    for t in range(chunk):
        for j in range(N_JT):
            ybuf[t * rows:(t + 1) * rows, j * LANE:(j + 1) * LANE] = y_ref[j, :, t * LANE:(t + 1) * LANE]
    y = _permute_f32(perm_t[chunk], ybuf[...]) + dskip_ref[...] * uf_ref[...]
    z = jax.nn.gelu(y)
    s = z * jax.nn.sigmoid(_dot(z.astype(BF16), wglu_ref[...]) + bglu_ref[...])
    s_n = _rms(s, nssm_ref[...])
    mixed = jnp.concatenate([a_n, s_n], axis=-1).astype(BF16)
    x2 = x1_ref[...] + _dot(mixed, wout_ref[...])
    x3 = _ffn(x2, nf_ref[...], wg_ref, wu_ref, wd_ref)
    gate = jax.nn.sigmoid(_dot(_rms(x3, nple_ref[...]).astype(BF16), wpg_ref[...]))
    x4 = x3 + gate * _dot(p_ref[...].astype(BF16), wpp_ref[...])
    out_ref[...] = _rms(x4, nfin_ref[...]) if final else x4


def _postmix(x1, att_planes, dils, y, uf, p, layer, weights, perms_t, nfin, tm, seq, chunk):
    m = x1.shape[0]
    per_seq = seq // tm
    final = nfin is not None
    row = lambda w: pl.BlockSpec((tm, w), lambda i: (i, 0))
    att_args, att_specs = [], []
    for d, (o, l) in zip(dils, att_planes):
        spec = row(D_ATT) if d == 1 else pl.BlockSpec((1, d, tm // d, D_ATT),
                                                       lambda i: (i // per_seq, 0, i % per_seq, 0))
        for a in ((o, l) if len(dils) > 1 else (o,)):
            att_args.append(a.reshape(m, D_ATT) if d == 1 else a)
            att_specs.append(spec)
    consts = list(weights) + [perms_t[d] for d in sorted((set(dils) | {chunk}) - {1})] + ([nfin] if final else [])
    args = [x1] + att_args + [y, uf, p] + consts
    in_specs = ([row(D_MODEL)] + att_specs
                + [pl.BlockSpec((N_JT, tm // chunk, chunk * LANE), lambda i: (0, i, 0)), row(D_SSM),
                   pl.BlockSpec((pl.Squeezed(), tm, D_PLE), lambda i: (layer, i, 0))]
                + [_const_spec(c.shape) for c in consts])
    return pl.pallas_call(
        functools.partial(_postmix_kernel, dils=tuple(dils), chunk=chunk, final=final),
        grid=(m // tm,),
        in_specs=in_specs,
        out_specs=row(D_MODEL),
        out_shape=jax.ShapeDtypeStruct((m, D_MODEL), F32),
        scratch_shapes=[pltpu.VMEM((tm, D_SSM), F32)],
        compiler_params=_params(1),
        name="postmix",
    )(*args)


def _t5_bucket(dist):
    dist = np.asarray(dist, dtype=np.int64)
    exact = N_BUCKETS // 2
    ratio = np.log(np.maximum(dist, 1) / exact) / np.log(MAX_DISTANCE / exact)
    large = np.minimum(exact + (ratio * (N_BUCKETS - exact)).astype(np.int64), N_BUCKETS - 1)
    return np.where(dist < exact, dist, large).astype(np.int32)


def _branch_bias(rel_bias, dil):
    return rel_bias[_t5_bucket(np.arange(N_STEP + 1) * dil)].T.astype(F32)


def _prompt_bias_row(rel_bias, dil):
    step = N_STEP - np.arange(2 * N_STEP)
    row = rel_bias[_t5_bucket(np.maximum(step, 0) * dil)].T.astype(F32)
    return jnp.where((step >= 0)[None], row, NEG)


def _sample_table(biases, step_of):
    n_br, t_new, cols = step_of.shape
    tabs = []
    for bi in range(n_br):
        vals = biases[bi][:, np.maximum(step_of[bi], 0)]
        tab = jnp.where((step_of[bi] >= 0)[None], vals, NEG)
        tabs.append(jnp.concatenate([tab, jnp.zeros((N_HEADS, Q_PAD - t_new, cols), F32)], axis=1))
    return jnp.stack(tabs)


def kernel(x_prompt, x_sample, p_prompt, p_sample, cache_k, cache_v, state_ssm_re, state_ssm_im,
           rel_bias, w_in, w_out, norm_mix, norm_att_out, norm_ssm_out, norm_ffn, ffn_w_gate,
           ffn_w_up, ffn_w_down, ssm_a_re, ssm_a_im, ssm_log_dt, ssm_b_re, ssm_b_im, ssm_c_re,
           ssm_c_im, ssm_d, w_glu, b_glu, norm_ple, w_ple_gate, w_ple_proj, norm_final):
    depth = w_in.shape[0]
    batch, seq, _ = x_prompt.shape
    dec_batch, dec_seq, _ = x_sample.shape
    cache_len = cache_k.shape[2]
    keep = min(BRANCHES[-1][0], seq)
    dils = tuple(d for _, d in BRANCHES)
    chunk_p = dils[-1]
    tm_p = 256
    tm_s = dec_batch * dec_seq
    assert seq % tm_p == 0 and chunk_p in dils and dec_seq % TOK_PER_PAIR == 0 and tm_s % 8 == 0

    xp = x_prompt.reshape(batch * seq, D_MODEL)
    xs = x_sample.reshape(tm_s, D_MODEL)
    biases = [_branch_bias(rel_bias, d) for d in dils]
    tabs_p = [_prompt_bias_row(rel_bias, d) for d in dils]
    tab_s = _sample_table(biases, _sample_steps(cache_len, dec_seq, tm_s))
    to_lanes = lambda a: a.transpose(0, 1, 3, 4, 2)
    from_lanes = lambda a: a.transpose(0, 1, 4, 2, 3)
    cache_kt, cache_vt = to_lanes(cache_k), to_lanes(cache_v)
    perms_p = {d: jnp.asarray(_perm_matrix(tm_p, d), BF16) for d in dils if d > 1}
    perms_pt = {d: jnp.asarray(_perm_matrix(tm_p, d).T, BF16) for d in dils if d > 1}
    perms_s = {dec_seq: jnp.asarray(_perm_matrix(tm_s, dec_seq), BF16)}
    perms_st = {dec_seq: jnp.asarray(_perm_matrix(tm_s, dec_seq).T, BF16)}
    row = lambda v: v.reshape(1, -1)
    bf = lambda a: a.astype(BF16)

    def layer_weights(i):
        pre = (row(norm_ffn[i, 0]), bf(ffn_w_gate[i, 0]), bf(ffn_w_up[i, 0]), bf(ffn_w_down[i, 0]),
               row(norm_mix[i]), bf(w_in[i]))
        post = (row(norm_att_out[i]), row(norm_ssm_out[i]), row(ssm_d[i]), bf(w_glu[i]), row(b_glu[i]),
                bf(w_out[i]), row(norm_ffn[i, 1]), bf(ffn_w_gate[i, 1]), bf(ffn_w_up[i, 1]),
                bf(ffn_w_down[i, 1]), row(norm_ple[i]), bf(w_ple_gate[i]), bf(w_ple_proj[i]))
        nfin = row(norm_final) if i == depth - 1 else None
        ssm_par = (ssm_a_re[i], ssm_a_im[i], ssm_log_dt[i], ssm_b_re[i], ssm_b_im[i], ssm_c_re[i], ssm_c_im[i])
        return pre, post, nfin, ssm_par

    k_news, v_news, rss, iss = [], [], [], []
    for i in range(depth):
        pre, post, nfin, ssm_par = layer_weights(i)
        x1, q, _, _, kf, vf, uf, uc = _premix(xs, pre, perms_s, tm_s, tm_s, tm_s, (), dec_seq)
        att = _attn_sample(q, kf[0], vf[0], cache_kt, cache_vt, i, tab_s, dec_batch, dec_seq)
        h0 = _state_to_tiles(state_ssm_re[i], state_ssm_im[i], 1)
        y, h_last = _ssm(uc, h0, _ssm_tables(*ssm_par, dec_seq), dec_seq, dec_batch, False)
        xs = _postmix(x1, [(att, None)], (1,), y, uf, p_sample.reshape(depth, tm_s, D_PLE), i, post, perms_st,
                      nfin, tm_s, tm_s, dec_seq)
        k_news.append(kf[0])
        v_news.append(vf[0])
        re, im = _tiles_to_state(h_last)
        rss.append(re)
        iss.append(im)

    assert depth >= 2
    windows = [(cache_kt, jnp.stack(k_news), dec_seq), (cache_vt, jnp.stack(v_news), dec_seq)]
    kp, vp, rp, ip = [], [], [], []
    for i in range(depth):
        pre, post, nfin, ssm_par = layer_weights(i)
        outs = list(_premix(xp, pre, perms_p, tm_p, seq, keep, dils[1:], chunk_p,
                            window=windows[i] if i < 2 else None))
        if i < 2:
            windows[i] = outs.pop()
        x1, q, k, v = outs[:4]
        kf, vf, uf, uc = outs[-4:]
        nat = lambda a: a.reshape(batch, 1, seq, D_ATT)
        qkv = [(nat(q), nat(k), nat(v))] + [tuple(outs[4 + 3 * n:7 + 3 * n]) for n in range(len(dils) - 1)]
        planes = [_attn_prompt_branch(*qkv[bi], tabs_p[bi]) for bi in range(len(dils))]
        zeros = jnp.zeros((batch, N_GROUPS, SSM_STATE), F32)
        y, h_last = _ssm(uc, _state_to_tiles(zeros, zeros, batch), _ssm_tables(*ssm_par, chunk_p),
                         chunk_p, seq // chunk_p, True)
        xp = _postmix(x1, planes, dils, y, uf, p_prompt.reshape(depth, batch * seq, D_PLE), i, post, perms_pt,
                      nfin, tm_p, seq, chunk_p)
        kp.append(kf)
        vp.append(vf)
        re, im = _tiles_to_state(h_last)
        rp.append(re)
        ip.append(im)

    return (xp.reshape(batch, seq, D_MODEL), xs.reshape(dec_batch, dec_seq, D_MODEL),
            from_lanes(jnp.stack(kp)), from_lanes(jnp.stack(vp)), jnp.stack(rp), jnp.stack(ip),
            from_lanes(windows[0]), from_lanes(windows[1]), jnp.stack(rss), jnp.stack(iss))
```

```python
import functools
import math

import numpy as np
import jax
import jax.numpy as jnp
from jax import lax
from jax.experimental import pallas as pl
from jax.experimental.pallas import tpu as pltpu

F32 = jnp.float32
BF16 = jnp.bfloat16

D_MODEL = 1024
D_ATT = 512
D_SSM = 512
HEAD_DIM = 64
N_HEADS = 8
SSM_GROUP = 16
N_GROUPS = 32
SSM_STATE = 64
BRANCHES = ((128, 1), (512, 4), (2048, 16))
N_STEP = 128
N_BUCKETS = 32
MAX_DISTANCE = 2048
D_FF = 2816
D_PLE = 256
D_IN = 3 * D_ATT + D_SSM
EPS = 1e-6
NEG = -1e30

LANE = 128
SUB = 8
MXU = 256
GROUPS_PER_TILE = LANE // SSM_GROUP
N_JT = D_SSM // LANE
PLANE = GROUPS_PER_TILE * SSM_STATE
TOK_PER_PAIR = MXU // LANE
VMEM_LIMIT = 56 * 1024 * 1024

assert all(w // d == N_STEP for w, d in BRANCHES)


def _rms(x, g):
    return x * lax.rsqrt(jnp.mean(x * x, axis=-1, keepdims=True) + EPS) * g


def _dot(a, b):
    return jnp.dot(a, b, preferred_element_type=F32)


def _ffn(x, g, wg_ref, wu_ref, wd_ref):
    h = _rms(x, g).astype(BF16)
    act = (jax.nn.silu(_dot(h, wg_ref[...])) * _dot(h, wu_ref[...])).astype(BF16)
    return x + 0.5 * _dot(act, wd_ref[...])


def _perm_matrix(n, d):
    p = np.zeros((n, n), np.float32)
    s, r = np.meshgrid(np.arange(n // d), np.arange(d), indexing="ij")
    p[(r * (n // d) + s).ravel(), (d * s + r).ravel()] = 1.0
    return p


def _permute_f32(pt_ref, x):
    hi = x.astype(BF16)
    lo = (x - hi.astype(F32)).astype(BF16)
    pt = pt_ref[...]
    return _dot(pt, hi) + _dot(pt, lo)


def _const_spec(shape):
    nd = len(shape)
    return pl.BlockSpec(shape, lambda *_: (0,) * nd, pipeline_mode=pl.Buffered(1))


def _params(n_axes):
    return pltpu.CompilerParams(dimension_semantics=("arbitrary",) * n_axes,
                                vmem_limit_bytes=VMEM_LIMIT)


def _window_part(old_ref, next_ref, new_ref, out_ref, part, n_parts, seq_idx, t_new):
    width, n_new = old_ref.shape[-1], new_ref.shape[-1]
    to_back = (LANE - t_new - seq_idx * t_new) % n_new
    tail = lax.broadcasted_iota(jnp.int32, (HEAD_DIM, LANE), 1) >= LANE - t_new
    for h in range(N_HEADS):
        shifted = pltpu.roll(old_ref[0, 0, h], width - t_new, 1)
        incoming = jnp.where(part == n_parts - 1,
                             pltpu.roll(new_ref[0, h], to_back, 1)[:, :LANE],
                             pltpu.roll(next_ref[0, 0, h], LANE - t_new, 1))
        out_ref[0, 0, h, :, :width - LANE] = shifted[:, :width - LANE]
        out_ref[0, 0, h, :, width - LANE:] = jnp.where(tail, incoming, shifted[:, width - LANE:])


def _premix_kernel(*refs, dils, chunk, per_seq, skip, window, n_prev):
    it = iter(refs)
    x_ref, nf_ref, wg_ref, wu_ref, wd_ref, nm_ref, win_ref = (next(it) for _ in range(7))
    perm = {d: next(it) for d in sorted(set(dils) | {chunk})}
    kprev_ref, vprev_ref = (next(it), next(it)) if n_prev else (None, None)
    window_in = [next(it) for _ in range(3)] if window else None
    x1_ref, q_ref, k_ref, v_ref = (next(it) for _ in range(4))
    strided = {d: (next(it), next(it), next(it)) for d in dils}
    kf_ref, vf_ref, uf_ref, uc_ref = (next(it) for _ in range(4))
    if window:
        n_parts, n_seq, t_new = window
        step = pl.program_id(0)
        _window_part(*window_in, next(it), step % n_parts, n_parts, (step // n_parts) % n_seq, t_new)

    x1 = _ffn(x_ref[...], nf_ref[...], wg_ref, wu_ref, wd_ref)
    x1_ref[...] = x1
    h = _rms(x1, nm_ref[...]).astype(BF16)
    qkvu = _dot(h, win_ref[...])
    q = (qkvu[:, :D_ATT] * (1.0 / math.sqrt(HEAD_DIM))).astype(BF16)
    kf = qkvu[:, D_ATT:2 * D_ATT]
    vf = qkvu[:, 2 * D_ATT:3 * D_ATT]
    uf = qkvu[:, 3 * D_ATT:]
    k, v, u = kf.astype(BF16), vf.astype(BF16), uf.astype(BF16)
    q_ref[...], k_ref[...], v_ref[...] = q, k, v
    uf_ref[...] = uf
    tm = x1.shape[0]

    @pl.when(pl.program_id(0) % per_seq >= skip)
    def _():
        for layer in range(n_prev):
            kf_ref[layer] = kprev_ref[layer]
            vf_ref[layer] = vprev_ref[layer]
        kf_ref[n_prev, 0] = kf.T.reshape(N_HEADS, HEAD_DIM, tm)
        vf_ref[n_prev, 0] = vf.T.reshape(N_HEADS, HEAD_DIM, tm)

    for d, outs in strided.items():
        for val, out in zip((q, k, v), outs):
            out[0] = _dot(perm[d][...], val).astype(BF16).reshape(d, tm // d, D_ATT)
    up = _dot(perm[chunk][...], u).astype(BF16)
    rows = tm // chunk
    for t in range(chunk):
        for j in range(N_JT):
            uc_ref[j, :, t * LANE:(t + 1) * LANE] = up[t * rows:(t + 1) * rows, j * LANE:(j + 1) * LANE]


def _premix(x, weights, perms, tm, seq, keep, dils, chunk, window=None, prev_state=None):
    m = x.shape[0]
    per_seq = seq // tm
    skip = (seq - keep) // tm
    assert keep % tm == 0 and tm % LANE == 0
    n_prev = 0 if prev_state is None else prev_state[0].shape[0]
    kept = lambda layers: pl.BlockSpec((layers, 1, N_HEADS, HEAD_DIM, tm),
                                       lambda i: (0, i // per_seq, 0, 0, jnp.maximum(i % per_seq - skip, 0)))
    kept_shape = jax.ShapeDtypeStruct((n_prev + 1, m // seq, N_HEADS, HEAD_DIM, keep), F32)
    row = lambda w: pl.BlockSpec((tm, w), lambda i: (i, 0))
    consts = list(weights) + [perms[d] for d in sorted(set(dils) | {chunk})]
    in_specs = [row(D_MODEL)] + [_const_spec(c.shape) for c in consts] + [kept(n_prev)] * (2 if n_prev else 0)
    out_specs = [row(D_MODEL), row(D_ATT), row(D_ATT), row(D_ATT)]
    out_shape = [jax.ShapeDtypeStruct((m, D_MODEL), F32)] + [jax.ShapeDtypeStruct((m, D_ATT), BF16)] * 3
    for d in dils:
        out_specs += [pl.BlockSpec((1, d, tm // d, D_ATT), lambda i: (i // per_seq, 0, i % per_seq, 0))] * 3
        out_shape += [jax.ShapeDtypeStruct((m // seq, d, seq // d, D_ATT), BF16)] * 3
    out_specs += [kept(n_prev + 1), kept(n_prev + 1), row(D_SSM),
                  pl.BlockSpec((N_JT, tm // chunk, chunk * LANE), lambda i: (0, i, 0))]
    out_shape += [kept_shape, kept_shape, jax.ShapeDtypeStruct((m, D_SSM), F32),
                  jax.ShapeDtypeStruct((N_JT, m // chunk, chunk * LANE), BF16)]
    args, window_cfg = [x] + consts + (list(prev_state) if n_prev else []), None
    if window is not None:
        old, new, t_new = window
        depth, n_seq, length = old.shape[0], old.shape[1], old.shape[-1]
        n_parts = (m // tm) // (depth * n_seq)
        width = length // n_parts
        assert n_parts * depth * n_seq == m // tm and width * n_parts == length and width % LANE == 0
        assert new.shape[-1] == n_seq * t_new and new.shape[-1] % LANE == 0 and t_new <= LANE
        tiles = width // LANE
        where = lambda i: (i // (n_parts * n_seq), (i // n_parts) % n_seq, i % n_parts)
        part = pl.BlockSpec((1, 1, N_HEADS, HEAD_DIM, width), lambda i: (*where(i)[:2], 0, 0, where(i)[2]))
        after = pl.BlockSpec((1, 1, N_HEADS, HEAD_DIM, LANE),
                             lambda i: (*where(i)[:2], 0, 0,
                                        jnp.minimum((where(i)[2] + 1) * tiles, length // LANE - 1)))
        args += [old, old, new]
        in_specs += [part, after, pl.BlockSpec((1,) + new.shape[1:], lambda i: (where(i)[0], 0, 0, 0))]
        out_specs.append(part)
        out_shape.append(jax.ShapeDtypeStruct(old.shape, old.dtype))
        window_cfg = (n_parts, n_seq, t_new)
    return pl.pallas_call(
        functools.partial(_premix_kernel, dils=tuple(dils), chunk=chunk, per_seq=per_seq, skip=skip,
                          window=window_cfg, n_prev=n_prev),
        grid=(m // tm,),
        in_specs=in_specs, out_specs=out_specs, out_shape=out_shape,
        compiler_params=_params(1),
        name="premix",
    )(*args)


def _attn_prompt_kernel(q_ref, kc_ref, kp_ref, vc_ref, vp_ref, bias_ref, o_ref, l_ref, kbuf, vbuf, tab_ref,
                        *, qb):
    nk = 2 * N_STEP

    @pl.when((pl.program_id(0) == 0) & (pl.program_id(1) == 0) & (pl.program_id(2) == 0))
    def _():
        no_prev = lax.broadcasted_iota(jnp.int32, (N_STEP, nk), 1) < N_STEP
        for h in range(N_HEADS):
            row = jnp.broadcast_to(bias_ref[h:h + 1, :], (N_STEP, nk))
            tab = pltpu.roll(row, 0, 1, stride=1, stride_axis=0)
            tab_ref[0, h] = tab
            tab_ref[1, h] = jnp.where(no_prev, NEG, tab)

    first = pl.program_id(2) == 0
    kbuf[0:N_STEP, :] = kp_ref[...]
    kbuf[N_STEP:, :] = kc_ref[...]
    vbuf[0:N_STEP, :] = vp_ref[...]
    vbuf[N_STEP:, :] = vc_ref[...]
    lo_half = lax.broadcasted_iota(jnp.int32, (N_STEP, LANE), 1) < HEAD_DIM
    keep_lo = jnp.where(lo_half, 1.0, 0.0).astype(BF16)
    keep_hi = jnp.where(lo_half, 0.0, 1.0).astype(BF16)

    def block(qi, carry):
        r0 = pl.multiple_of(qi * N_STEP, N_STEP)
        which = jnp.where(first & (qi == 0), 1, 0)
        for hp in range(N_HEADS // 2):
            lanes = slice(hp * LANE, (hp + 1) * LANE)
            qp = q_ref[pl.ds(r0, N_STEP), lanes]
            kp = kbuf[pl.ds(r0, nk), lanes]
            vp = vbuf[pl.ds(r0, nk), lanes]
            o_pair, l_pair = None, None
            for hh in range(2):
                qm = qp * (keep_lo if hh == 0 else keep_hi)
                s = lax.dot_general(qm, kp, (((1,), (1,)), ((), ())), preferred_element_type=F32)
                s = s + tab_ref[which, 2 * hp + hh]
                m = jnp.max(s, axis=-1, keepdims=True)
                e = jnp.exp(s - m)
                den = jnp.sum(e, axis=-1, keepdims=True)
                o = _dot(e.astype(BF16), vp) / den
                lse = jnp.broadcast_to(m + jnp.log(den), (N_STEP, LANE))
                if hh == 0:
                    o_pair, l_pair = o, lse
                else:
                    o_pair = jnp.where(lo_half, o_pair, o)
                    l_pair = jnp.where(lo_half, l_pair, lse)
            o_ref[pl.ds(r0, N_STEP), lanes] = o_pair
            l_ref[pl.ds(r0, N_STEP), lanes] = l_pair
        return carry

    lax.fori_loop(0, qb // N_STEP, block, 0)


def _attn_prompt_branch(q, k, v, bias_row):
    batch, dil, sub, _ = q.shape
    qb = min(sub, 512)
    assert sub % qb == 0 and qb % N_STEP == 0
    per = qb // N_STEP
    sq = pl.Squeezed()
    cur = pl.BlockSpec((sq, sq, qb, D_ATT), lambda b, r, i: (b, r, i, 0))
    prev = pl.BlockSpec((sq, sq, N_STEP, D_ATT), lambda b, r, i: (b, r, jnp.maximum(i * per - 1, 0), 0))
    return pl.pallas_call(
        functools.partial(_attn_prompt_kernel, qb=qb),
        grid=(batch, dil, sub // qb),
        in_specs=[cur, cur, prev, cur, prev, _const_spec(bias_row.shape)],
        out_specs=[cur, cur],
        out_shape=[jax.ShapeDtypeStruct(q.shape, F32)] * 2,
        scratch_shapes=[pltpu.VMEM((qb + N_STEP, D_ATT), BF16)] * 2
                       + [pltpu.VMEM((2, N_HEADS, N_STEP, 2 * N_STEP), F32)],
        compiler_params=_params(3),
        name=f"attn_prompt_d{dil}",
    )(q, k, k, v, v, bias_row)


Q_PAD = 8


def _sample_steps(cache_len, t_new, n_new):
    step_of = np.full((len(BRANCHES), t_new, cache_len + n_new), -1, np.int32)
    for bi, (w, d) in enumerate(BRANCHES):
        for i in range(t_new):
            for j in range(w // d + 1):
                ext = cache_len + i - j * d
                assert ext >= 0
                step_of[bi, i, ext] = j
    return step_of


def _attn_sample_kernel(q_ref, kn_ref, vn_ref, kc_ref, vc_ref, tab_ref, o_ref, qs, *, t_new):
    n_new = kn_ref.shape[2]
    to_front = (n_new - pl.program_id(0) * t_new) % n_new
    q = q_ref[0].astype(F32)
    qs[...] = jnp.zeros(qs.shape, F32)
    for h in range(N_HEADS):
        qs[h, 0:t_new, :] = q[:, h * HEAD_DIM:(h + 1) * HEAD_DIM]
        qh = qs[h].astype(BF16)
        kn = pltpu.roll(kn_ref[h], to_front, 1).astype(BF16)
        vn = pltpu.roll(vn_ref[h], to_front, 1).astype(BF16)
        s = jnp.concatenate([_dot(qh, kc_ref[0, h].astype(BF16)), _dot(qh, kn)], axis=-1)
        logits = [s + tab_ref[bi, h] for bi in range(len(BRANCHES))]
        m = jnp.max(functools.reduce(jnp.maximum, logits), axis=-1, keepdims=True)
        e = sum(jnp.exp(l - m) for l in logits)
        den = jnp.sum(e, axis=-1, keepdims=True)
        eb = e.astype(BF16)
        nt = (((1,), (1,)), ((), ()))
        o = (lax.dot_general(eb[:, :-n_new], vc_ref[0, h].astype(BF16), nt, preferred_element_type=F32)
             + lax.dot_general(eb[:, -n_new:], vn, nt, preferred_element_type=F32)) / den
        o_ref[0, :, h * HEAD_DIM:(h + 1) * HEAD_DIM] = o[:t_new]


def _attn_sample(q, k_new, v_new, cache_k, cache_v, layer, tab, batch, t_new):
    cache_len = cache_k.shape[-1]
    tok = pl.BlockSpec((1, t_new, D_ATT), lambda b: (b, 0, 0))
    slab = pl.BlockSpec((pl.Squeezed(), 1, N_HEADS, HEAD_DIM, cache_len), lambda b: (layer, b, 0, 0, 0))
    out = pl.pallas_call(
        functools.partial(_attn_sample_kernel, t_new=t_new),
        grid=(batch,),
        in_specs=[tok, _const_spec(k_new.shape), _const_spec(v_new.shape), slab, slab, _const_spec(tab.shape)],
        out_specs=tok,
        out_shape=jax.ShapeDtypeStruct((batch, t_new, D_ATT), F32),
        scratch_shapes=[pltpu.VMEM((N_HEADS, Q_PAD, HEAD_DIM), F32)],
        compiler_params=_params(1),
        name="attn_sample",
    )(q.reshape(batch, t_new, D_ATT), k_new, v_new, cache_k, cache_v, tab)
    return out.reshape(batch * t_new, D_ATT)


def _ssm_kernel(u_ref, h0_ref, w_ref, pw_ref, qw_ref, a_ref, y_ref, hl_ref, z_buf, hb_buf,
                *, chunk, scan_rows):
    n_pair = chunk // TOK_PER_PAIR
    ub = u_ref[0]
    z = _dot(ub, pw_ref[0])
    gr, gi = a_ref[0, 0], a_ref[0, 1]
    h0 = h0_ref[0, 0]

    def times(m, xr, xi):
        cr, ci = gr[m:m + 1], gi[m:m + 1]
        return cr * xr - ci * xi, cr * xi + ci * xr

    if scan_rows:
        rows = ub.shape[0]
        in_group = lax.broadcasted_iota(jnp.int32, (rows, PLANE), 0) % SUB

        def below(x, sh):
            return jnp.where(in_group >= sh, pltpu.roll(x, sh, 0), 0.0)

        lr, li = z[:, 0:PLANE], z[:, PLANE:]
        sh = 1
        while sh < SUB:
            pr, pi = times(sh, below(lr, sh), below(li, sh))
            lr, li = lr + pr, li + pi
            sh *= 2
        z_buf[:, 0:PLANE] = lr
        z_buf[:, PLANE:] = li

        def group(b, h):
            hr, hi = h
            r0 = pl.multiple_of(b * SUB, SUB)
            hb_buf[pl.ds(r0, SUB), 0:PLANE] = jnp.broadcast_to(hr, (SUB, PLANE))
            hb_buf[pl.ds(r0, SUB), PLANE:] = jnp.broadcast_to(hi, (SUB, PLANE))
            pr, pi = times(SUB, hr, hi)
            return (pr + z_buf[pl.ds(r0 + SUB - 1, 1), 0:PLANE], pi + z_buf[pl.ds(r0 + SUB - 1, 1), PLANE:])

        hr, hi = lax.fori_loop(0, rows // SUB, group, (h0[:, 0:PLANE], h0[:, PLANE:]))
        hl_ref[0, 0, :, 0:PLANE] = hr
        hl_ref[0, 0, :, PLANE:] = hi
        pw_r = jnp.broadcast_to(gr[0:SUB], (rows // SUB, SUB, PLANE)).reshape(rows, PLANE)
        pw_i = jnp.broadcast_to(gi[0:SUB], (rows // SUB, SUB, PLANE)).reshape(rows, PLANE)
        sr, si = hb_buf[:, 0:PLANE], hb_buf[:, PLANE:]
        hb = jnp.concatenate([pw_r * sr - pw_i * si + below(lr, 1), pw_r * si + pw_i * sr + below(li, 1)],
                             axis=-1).astype(BF16)
    else:
        hr, hi = h0[:, 0:PLANE], h0[:, PLANE:]
        pr, pi = times(1, hr, hi)
        hl_ref[0, 0, :, 0:PLANE] = pr + z[:, 0:PLANE]
        hl_ref[0, 0, :, PLANE:] = pi + z[:, PLANE:]
        hb = h0.astype(BF16)
    for b in range(n_pair):
        acc = _dot(hb, qw_ref[0, b])
        for a in range(b + 1):
            acc = acc + _dot(ub[:, a * MXU:(a + 1) * MXU], w_ref[0, b - a])
        y_ref[0, :, b * MXU:(b + 1) * MXU] = acc


def _ssm(uc, h0, tabs, chunk, rows_per_block, scan_rows):
    w, pw, qw, a_pow = tabs
    rows, width = uc.shape[1:]
    n_blocks = rows // rows_per_block
    seqs = h0.shape[2]
    return pl.pallas_call(
        functools.partial(_ssm_kernel, chunk=chunk, scan_rows=scan_rows),
        grid=(N_JT, n_blocks),
        in_specs=[pl.BlockSpec((1, rows_per_block, width), lambda j, b: (j, b, 0)),
                  pl.BlockSpec((1, 1, seqs, 2 * PLANE), lambda j, b: (j, b, 0, 0)),
                  pl.BlockSpec((1,) + w.shape[1:], lambda j, b: (j, 0, 0, 0)),
                  pl.BlockSpec((1,) + pw.shape[1:], lambda j, b: (j, 0, 0)),
                  pl.BlockSpec((1,) + qw.shape[1:], lambda j, b: (j, 0, 0, 0)),
                  pl.BlockSpec((1,) + a_pow.shape[1:], lambda j, b: (j, 0, 0, 0))],
        out_specs=[pl.BlockSpec((1, rows_per_block, width), lambda j, b: (j, b, 0)),
                   pl.BlockSpec((1, 1, seqs, 2 * PLANE), lambda j, b: (j, b, 0, 0))],
        out_shape=[jax.ShapeDtypeStruct(uc.shape, F32),
                   jax.ShapeDtypeStruct(h0.shape, F32)],
        scratch_shapes=[pltpu.VMEM((rows_per_block, 2 * PLANE), F32)] * 2,
        compiler_params=_params(2),
        name=f"ssm_chunk{chunk}",
    )(uc, h0, w, pw, qw, a_pow)


def _ssm_tables(a_re, a_im, log_dt, b_re, b_im, c_re, c_im, chunk):
    hi = lax.Precision.HIGHEST
    dt = jnp.exp(log_dt)[:, None]
    den = a_re * a_re + a_im * a_im
    er, ang = jnp.exp(a_re * dt), a_im * dt
    ab_re, ab_im = er * jnp.cos(ang), er * jnp.sin(ang)
    nr, ni = ab_re - 1.0, ab_im
    f_re, f_im = (nr * a_re + ni * a_im) / den, (ni * a_re - nr * a_im) / den
    bb_re = f_re[..., None] * b_re - f_im[..., None] * b_im
    bb_im = f_re[..., None] * b_im + f_im[..., None] * b_re

    def power(taus):
        taus = jnp.asarray(taus, F32)[:, None, None]
        mag, ang_t = jnp.exp(a_re * dt * taus), a_im * dt * taus
        return mag * jnp.cos(ang_t), mag * jnp.sin(ang_t)

    p_re, p_im = power(np.arange(chunk + 1))
    pr, pi = (x[..., None] for x in power(chunk - 1 - np.arange(chunk)))
    s_re, s_im = pr * bb_re - pi * bb_im, pr * bb_im + pi * bb_re
    qr, qi = p_re[1:, :, None, :], p_im[1:, :, None, :]
    e_re, e_im = c_re * qr - c_im * qi, c_re * qi + c_im * qr
    lr, li = p_re[:chunk, :, None, :], p_im[:chunk, :, None, :]
    cl_re, cl_im = c_re * lr - c_im * li, c_re * li + c_im * lr
    kern = jnp.einsum("tgon,gni->tgoi", jnp.concatenate([cl_re, -cl_im], axis=-1),
                      jnp.concatenate([bb_re, bb_im], axis=1), precision=hi)

    eye = jnp.eye(GROUPS_PER_TILE, dtype=F32)
    n_pair = chunk // TOK_PER_PAIR
    tile = lambda x: x.reshape((x.shape[0], N_JT, GROUPS_PER_TILE) + x.shape[2:])
    kern_t = tile(kern)
    kern_t = jnp.concatenate([kern_t, jnp.zeros_like(kern_t[:1])], axis=0)
    lag = (TOK_PER_PAIR * np.arange(n_pair)[:, None, None]
           + np.arange(TOK_PER_PAIR)[None, None, :] - np.arange(TOK_PER_PAIR)[None, :, None])
    kl = kern_t[lag]
    w = jnp.einsum("dpqjgoi,gh->jdpgiqho", kl, eye)
    w = w.reshape(N_JT, n_pair, MXU, MXU)
    gi = GROUPS_PER_TILE
    row_g = (np.arange(chunk * LANE) // SSM_GROUP) % gi
    mask_p = jnp.asarray(row_g[:, None] == (np.arange(PLANE) // SSM_STATE)[None], F32)

    def state_image(s):
        a = tile(s).transpose(1, 0, 2, 4, 3).reshape(N_JT, chunk * LANE, SSM_STATE)
        return jnp.tile(a, (1, 1, gi)) * mask_p

    pw = jnp.concatenate([state_image(s_re), state_image(s_im)], axis=-1)
    mask_q = jnp.asarray((np.arange(PLANE) // SSM_STATE)[:, None] == (np.arange(LANE) // SSM_GROUP)[None], F32)

    def readout(e):
        a = tile(e).transpose(1, 0, 2, 4, 3)
        a = a.reshape(N_JT, n_pair, TOK_PER_PAIR, gi, SSM_STATE, SSM_GROUP).transpose(0, 1, 3, 4, 2, 5)
        a = a.reshape(N_JT, n_pair, PLANE, TOK_PER_PAIR, SSM_GROUP)
        return jnp.concatenate([jnp.tile(a[:, :, :, t], (1, 1, 1, gi)) * mask_q
                                for t in range(TOK_PER_PAIR)], axis=-1)

    qw = jnp.concatenate([readout(e_re), readout(-e_im)], axis=2)
    a_pow = jnp.stack([x.reshape(SUB + 1, N_JT, PLANE).transpose(1, 0, 2)
                       for x in power(chunk * np.arange(SUB + 1))], axis=1)
    return w.astype(BF16), pw.astype(BF16), qw.astype(BF16), a_pow


def _state_to_tiles(re, im, n_blocks):
    n_seq = re.shape[0]
    t = lambda x: x.reshape(n_blocks, n_seq // n_blocks, N_JT, PLANE).transpose(2, 0, 1, 3)
    return jnp.concatenate([t(re), t(im)], axis=-1)


def _tiles_to_state(h):
    n_seq = h.shape[1] * h.shape[2]
    t = lambda x: x.transpose(1, 2, 0, 3).reshape(n_seq, N_GROUPS, SSM_STATE)
    return t(h[..., :PLANE]), t(h[..., PLANE:])


def _postmix_kernel(*refs, dils, chunk, final):
    it = iter(refs)
    x1_ref = next(it)
    att_refs = [(next(it), next(it)) for _ in dils] if len(dils) > 1 else [(next(it), None)]
    y_ref, uf_ref, p_ref = next(it), next(it), next(it)
    (natt_ref, nssm_ref, dskip_ref, wglu_ref, bglu_ref, wout_ref, nf_ref, wg_ref, wu_ref, wd_ref,
     nple_ref, wpg_ref, wpp_ref) = (next(it) for _ in range(13))
    perm_t = {d: next(it) for d in sorted((set(dils) | {chunk}) - {1})}
    nfin_ref = next(it) if final else None
    out_ref, ybuf = next(it), next(it)
    tm = x1_ref.shape[0]

    def natural(ref, d):
        val = ref[...] if d == 1 else ref[0].reshape(tm, D_ATT)
        return val if d == 1 else _permute_f32(perm_t[d], val)

    if len(dils) == 1:
        att = att_refs[0][0][...]
    else:
        outs = [natural(o, d) for d, (o, _) in zip(dils, att_refs)]
        lses = [natural(l, d) for d, (_, l) in zip(dils, att_refs)]
        top = functools.reduce(jnp.maximum, lses)
        ws = [jnp.exp(l - top) for l in lses]
        att = sum(w * o for w, o in zip(ws, outs)) / sum(ws)
    a_n = _rms(att, natt_ref[...])
    rows = tm // chunk
    for t in range(chunk):
        for j in range(N_JT):
            ybuf[t * rows:(t + 1) * rows, j * LANE:(j + 1) * LANE] = y_ref[j, :, t * LANE:(t + 1) * LANE]
    y = _permute_f32(perm_t[chunk], ybuf[...]) + dskip_ref[...] * uf_ref[...]
    z = jax.nn.gelu(y)
    s = z * jax.nn.sigmoid(_dot(z.astype(BF16), wglu_ref[...]) + bglu_ref[...])
    s_n = _rms(s, nssm_ref[...])
    mixed = jnp.concatenate([a_n, s_n], axis=-1).astype(BF16)
    x2 = x1_ref[...] + _dot(mixed, wout_ref[...])
    x3 = _ffn(x2, nf_ref[...], wg_ref, wu_ref, wd_ref)
    gate = jax.nn.sigmoid(_dot(_rms(x3, nple_ref[...]).astype(BF16), wpg_ref[...]))
    x4 = x3 + gate * _dot(p_ref[...].astype(BF16), wpp_ref[...])
    out_ref[...] = _rms(x4, nfin_ref[...]) if final else x4


def _postmix(x1, att_planes, dils, y, uf, p, layer, weights, perms_t, nfin, tm, seq, chunk):
    m = x1.shape[0]
    per_seq = seq // tm
    final = nfin is not None
    row = lambda w: pl.BlockSpec((tm, w), lambda i: (i, 0))
    att_args, att_specs = [], []
    for d, (o, l) in zip(dils, att_planes):
        spec = row(D_ATT) if d == 1 else pl.BlockSpec((1, d, tm // d, D_ATT),
                                                       lambda i: (i // per_seq, 0, i % per_seq, 0))
        for a in ((o, l) if len(dils) > 1 else (o,)):
            att_args.append(a.reshape(m, D_ATT) if d == 1 else a)
            att_specs.append(spec)
    consts = list(weights) + [perms_t[d] for d in sorted((set(dils) | {chunk}) - {1})] + ([nfin] if final else [])
    args = [x1] + att_args + [y, uf, p] + consts
    in_specs = ([row(D_MODEL)] + att_specs
                + [pl.BlockSpec((N_JT, tm // chunk, chunk * LANE), lambda i: (0, i, 0)), row(D_SSM),
                   pl.BlockSpec((pl.Squeezed(), tm, D_PLE), lambda i: (layer, i, 0))]
                + [_const_spec(c.shape) for c in consts])
    return pl.pallas_call(
        functools.partial(_postmix_kernel, dils=tuple(dils), chunk=chunk, final=final),
        grid=(m // tm,),
        in_specs=in_specs,
        out_specs=row(D_MODEL),
        out_shape=jax.ShapeDtypeStruct((m, D_MODEL), F32),
        scratch_shapes=[pltpu.VMEM((tm, D_SSM), F32)],
        compiler_params=_params(1),
        name="postmix",
    )(*args)


def _t5_bucket(dist):
    dist = np.asarray(dist, dtype=np.int64)
    exact = N_BUCKETS // 2
    ratio = np.log(np.maximum(dist, 1) / exact) / np.log(MAX_DISTANCE / exact)
    large = np.minimum(exact + (ratio * (N_BUCKETS - exact)).astype(np.int64), N_BUCKETS - 1)
    return np.where(dist < exact, dist, large).astype(np.int32)


def _branch_bias(rel_bias, dil):
    return rel_bias[_t5_bucket(np.arange(N_STEP + 1) * dil)].T.astype(F32)


def _prompt_bias_row(rel_bias, dil):
    step = N_STEP - np.arange(2 * N_STEP)
    row = rel_bias[_t5_bucket(np.maximum(step, 0) * dil)].T.astype(F32)
    return jnp.where((step >= 0)[None], row, NEG)


def _sample_table(biases, step_of):
    n_br, t_new, cols = step_of.shape
    tabs = []
    for bi in range(n_br):
        vals = biases[bi][:, np.maximum(step_of[bi], 0)]
        tab = jnp.where((step_of[bi] >= 0)[None], vals, NEG)
        tabs.append(jnp.concatenate([tab, jnp.zeros((N_HEADS, Q_PAD - t_new, cols), F32)], axis=1))
    return jnp.stack(tabs)


def kernel(x_prompt, x_sample, p_prompt, p_sample, cache_k, cache_v, state_ssm_re, state_ssm_im,
           rel_bias, w_in, w_out, norm_mix, norm_att_out, norm_ssm_out, norm_ffn, ffn_w_gate,
           ffn_w_up, ffn_w_down, ssm_a_re, ssm_a_im, ssm_log_dt, ssm_b_re, ssm_b_im, ssm_c_re,
           ssm_c_im, ssm_d, w_glu, b_glu, norm_ple, w_ple_gate, w_ple_proj, norm_final):
    depth = w_in.shape[0]
    batch, seq, _ = x_prompt.shape
    dec_batch, dec_seq, _ = x_sample.shape
    cache_len = cache_k.shape[2]
    keep = min(BRANCHES[-1][0], seq)
    dils = tuple(d for _, d in BRANCHES)
    chunk_p = dils[-1]
    tm_p = 256
    tm_s = dec_batch * dec_seq
    assert seq % tm_p == 0 and chunk_p in dils and dec_seq % TOK_PER_PAIR == 0 and tm_s % 8 == 0

    xp = x_prompt.reshape(batch * seq, D_MODEL)
    xs = x_sample.reshape(tm_s, D_MODEL)
    biases = [_branch_bias(rel_bias, d) for d in dils]
    tabs_p = [_prompt_bias_row(rel_bias, d) for d in dils]
    tab_s = _sample_table(biases, _sample_steps(cache_len, dec_seq, tm_s))
    to_lanes = lambda a: a.transpose(0, 1, 3, 4, 2)
    from_lanes = lambda a: a.transpose(0, 1, 4, 2, 3)
    cache_kt, cache_vt = to_lanes(cache_k), to_lanes(cache_v)
    perms_p = {d: jnp.asarray(_perm_matrix(tm_p, d), BF16) for d in dils if d > 1}
    perms_pt = {d: jnp.asarray(_perm_matrix(tm_p, d).T, BF16) for d in dils if d > 1}
    perms_s = {dec_seq: jnp.asarray(_perm_matrix(tm_s, dec_seq), BF16)}
    perms_st = {dec_seq: jnp.asarray(_perm_matrix(tm_s, dec_seq).T, BF16)}
    row = lambda v: v.reshape(1, -1)
    bf = lambda a: a.astype(BF16)

    def layer_weights(i):
        pre = (row(norm_ffn[i, 0]), bf(ffn_w_gate[i, 0]), bf(ffn_w_up[i, 0]), bf(ffn_w_down[i, 0]),
               row(norm_mix[i]), bf(w_in[i]))
        post = (row(norm_att_out[i]), row(norm_ssm_out[i]), row(ssm_d[i]), bf(w_glu[i]), row(b_glu[i]),
                bf(w_out[i]), row(norm_ffn[i, 1]), bf(ffn_w_gate[i, 1]), bf(ffn_w_up[i, 1]),
                bf(ffn_w_down[i, 1]), row(norm_ple[i]), bf(w_ple_gate[i]), bf(w_ple_proj[i]))
        nfin = row(norm_final) if i == depth - 1 else None
        ssm_par = (ssm_a_re[i], ssm_a_im[i], ssm_log_dt[i], ssm_b_re[i], ssm_b_im[i], ssm_c_re[i], ssm_c_im[i])
        return pre, post, nfin, ssm_par

    k_news, v_news, rss, iss = [], [], [], []
    for i in range(depth):
        pre, post, nfin, ssm_par = layer_weights(i)
        x1, q, _, _, kf, vf, uf, uc = _premix(xs, pre, perms_s, tm_s, tm_s, tm_s, (), dec_seq)
        att = _attn_sample(q, kf[0, 0], vf[0, 0], cache_kt, cache_vt, i, tab_s, dec_batch, dec_seq)
        h0 = _state_to_tiles(state_ssm_re[i], state_ssm_im[i], 1)
        y, h_last = _ssm(uc, h0, _ssm_tables(*ssm_par, dec_seq), dec_seq, dec_batch, False)
        xs = _postmix(x1, [(att, None)], (1,), y, uf, p_sample.reshape(depth, tm_s, D_PLE), i, post, perms_st,
                      nfin, tm_s, tm_s, dec_seq)
        k_news.append(kf[0, 0])
        v_news.append(vf[0, 0])
        re, im = _tiles_to_state(h_last)
        rss.append(re)
        iss.append(im)

    assert depth >= 2
    windows = [(cache_kt, jnp.stack(k_news), dec_seq), (cache_vt, jnp.stack(v_news), dec_seq)]
    kv_state, rp, ip = None, [], []
    for i in range(depth):
        pre, post, nfin, ssm_par = layer_weights(i)
        outs = list(_premix(xp, pre, perms_p, tm_p, seq, keep, dils[1:], chunk_p,
                            window=windows[i] if i < 2 else None, prev_state=kv_state))
        if i < 2:
            windows[i] = outs.pop()
        x1, q, k, v = outs[:4]
        kf, vf, uf, uc = outs[-4:]
        kv_state = (kf, vf)
        nat = lambda a: a.reshape(batch, 1, seq, D_ATT)
        qkv = [(nat(q), nat(k), nat(v))] + [tuple(outs[4 + 3 * n:7 + 3 * n]) for n in range(len(dils) - 1)]
        planes = [_attn_prompt_branch(*qkv[bi], tabs_p[bi]) for bi in range(len(dils))]
        zeros = jnp.zeros((batch, N_GROUPS, SSM_STATE), F32)
        y, h_last = _ssm(uc, _state_to_tiles(zeros, zeros, batch), _ssm_tables(*ssm_par, chunk_p),
                         chunk_p, seq // chunk_p, True)
        xp = _postmix(x1, planes, dils, y, uf, p_prompt.reshape(depth, batch * seq, D_PLE), i, post, perms_pt,
                      nfin, tm_p, seq, chunk_p)
        re, im = _tiles_to_state(h_last)
        rp.append(re)
        ip.append(im)

    return (xp.reshape(batch, seq, D_MODEL), xs.reshape(dec_batch, dec_seq, D_MODEL),
            from_lanes(kv_state[0]), from_lanes(kv_state[1]), jnp.stack(rp), jnp.stack(ip),
            from_lanes(windows[0]), from_lanes(windows[1]), jnp.stack(rss), jnp.stack(iss))
```

```python
import functools
import math

import numpy as np
import jax
import jax.numpy as jnp
from jax import lax
from jax.experimental import pallas as pl
from jax.experimental.pallas import tpu as pltpu

F32 = jnp.float32
BF16 = jnp.bfloat16

D_MODEL = 1024
D_ATT = 512
D_SSM = 512
HEAD_DIM = 64
N_HEADS = 8
SSM_GROUP = 16
N_GROUPS = 32
SSM_STATE = 64
BRANCHES = ((128, 1), (512, 4), (2048, 16))
N_STEP = 128
N_BUCKETS = 32
MAX_DISTANCE = 2048
D_FF = 2816
D_PLE = 256
D_IN = 3 * D_ATT + D_SSM
EPS = 1e-6
NEG = -1e30

LANE = 128
SUB = 8
MXU = 256
GROUPS_PER_TILE = LANE // SSM_GROUP
N_JT = D_SSM // LANE
PLANE = GROUPS_PER_TILE * SSM_STATE
TOK_PER_PAIR = MXU // LANE
VMEM_LIMIT = 56 * 1024 * 1024

assert all(w // d == N_STEP for w, d in BRANCHES)


def _rms(x, g):
    return x * lax.rsqrt(jnp.mean(x * x, axis=-1, keepdims=True) + EPS) * g


def _dot(a, b):
    return jnp.dot(a, b, preferred_element_type=F32)


def _ffn(x, g, wg_ref, wu_ref, wd_ref):
    h = _rms(x, g).astype(BF16)
    act = (jax.nn.silu(_dot(h, wg_ref[...])) * _dot(h, wu_ref[...])).astype(BF16)
    return x + 0.5 * _dot(act, wd_ref[...])


def _perm_matrix(n, d):
    p = np.zeros((n, n), np.float32)
    s, r = np.meshgrid(np.arange(n // d), np.arange(d), indexing="ij")
    p[(r * (n // d) + s).ravel(), (d * s + r).ravel()] = 1.0
    return p


def _permute_f32(pt_ref, x):
    hi = x.astype(BF16)
    lo = (x - hi.astype(F32)).astype(BF16)
    pt = pt_ref[...]
    return _dot(pt, hi) + _dot(pt, lo)


def _const_spec(shape, lead=()):
    tail = tuple(shape[len(lead):])
    return pl.BlockSpec((pl.Squeezed(),) * len(lead) + tail, lambda *_: tuple(lead) + (0,) * len(tail),
                        pipeline_mode=pl.Buffered(1))


def _split_consts(consts):
    pairs = [c if isinstance(c, tuple) else (c, ()) for c in consts]
    return [a for a, _ in pairs], [_const_spec(a.shape, lead) for a, lead in pairs]


def _params(n_axes):
    return pltpu.CompilerParams(dimension_semantics=("arbitrary",) * n_axes,
                                vmem_limit_bytes=VMEM_LIMIT)


def _window_part(old_ref, next_ref, new_ref, out_ref, part, n_parts, seq_idx, t_new):
    width, n_new = old_ref.shape[-1], new_ref.shape[-1]
    to_back = (LANE - t_new - seq_idx * t_new) % n_new
    tail = lax.broadcasted_iota(jnp.int32, (HEAD_DIM, LANE), 1) >= LANE - t_new
    for h in range(N_HEADS):
        shifted = pltpu.roll(old_ref[0, 0, h], width - t_new, 1)
        incoming = jnp.where(part == n_parts - 1,
                             pltpu.roll(new_ref[0, h], to_back, 1)[:, :LANE],
                             pltpu.roll(next_ref[0, 0, h], LANE - t_new, 1))
        out_ref[0, 0, h, :, :width - LANE] = shifted[:, :width - LANE]
        out_ref[0, 0, h, :, width - LANE:] = jnp.where(tail, incoming, shifted[:, width - LANE:])


def _premix_kernel(*refs, dils, chunk, per_seq, skip, window, n_prev):
    it = iter(refs)
    x_ref, nf_ref, wg_ref, wu_ref, wd_ref, nm_ref, win_ref = (next(it) for _ in range(7))
    perm = {d: next(it) for d in sorted(set(dils) | {chunk})}
    kprev_ref, vprev_ref = (next(it), next(it)) if n_prev else (None, None)
    window_in = [next(it) for _ in range(3)] if window else None
    x1_ref, q_ref, k_ref, v_ref = (next(it) for _ in range(4))
    strided = {d: (next(it), next(it), next(it)) for d in dils}
    kf_ref, vf_ref, uf_ref, uc_ref = (next(it) for _ in range(4))
    if window:
        n_parts, n_seq, t_new = window
        step = pl.program_id(0)
        _window_part(*window_in, next(it), step % n_parts, n_parts, (step // n_parts) % n_seq, t_new)

    x1 = _ffn(x_ref[...], nf_ref[...], wg_ref, wu_ref, wd_ref)
    x1_ref[...] = x1
    h = _rms(x1, nm_ref[...]).astype(BF16)
    qkvu = _dot(h, win_ref[...])
    q = (qkvu[:, :D_ATT] * (1.0 / math.sqrt(HEAD_DIM))).astype(BF16)
    kf = qkvu[:, D_ATT:2 * D_ATT]
    vf = qkvu[:, 2 * D_ATT:3 * D_ATT]
    uf = qkvu[:, 3 * D_ATT:]
    k, v, u = kf.astype(BF16), vf.astype(BF16), uf.astype(BF16)
    q_ref[...], k_ref[...], v_ref[...] = q, k, v
    uf_ref[...] = uf
    tm = x1.shape[0]

    @pl.when(pl.program_id(0) % per_seq >= skip)
    def _():
        for layer in range(n_prev):
            kf_ref[layer] = kprev_ref[layer]
            vf_ref[layer] = vprev_ref[layer]
        kf_ref[n_prev, 0] = kf.T.reshape(N_HEADS, HEAD_DIM, tm)
        vf_ref[n_prev, 0] = vf.T.reshape(N_HEADS, HEAD_DIM, tm)

    for d, outs in strided.items():
        for val, out in zip((q, k, v), outs):
            out[0] = _dot(perm[d][...], val).astype(BF16).reshape(d, tm // d, D_ATT)
    up = _dot(perm[chunk][...], u).astype(BF16)
    rows = tm // chunk
    for t in range(chunk):
        for j in range(N_JT):
            uc_ref[j, :, t * LANE:(t + 1) * LANE] = up[t * rows:(t + 1) * rows, j * LANE:(j + 1) * LANE]


def _premix(x, weights, perms, tm, seq, keep, dils, chunk, window=None, prev_state=None):
    m = x.shape[0]
    per_seq = seq // tm
    skip = (seq - keep) // tm
    assert keep % tm == 0 and tm % LANE == 0
    n_prev = 0 if prev_state is None else prev_state[0].shape[0]
    kept = lambda layers: pl.BlockSpec((layers, 1, N_HEADS, HEAD_DIM, tm),
                                       lambda i: (0, i // per_seq, 0, 0, jnp.maximum(i % per_seq - skip, 0)))
    kept_shape = jax.ShapeDtypeStruct((n_prev + 1, m // seq, N_HEADS, HEAD_DIM, keep), F32)
    row = lambda w: pl.BlockSpec((tm, w), lambda i: (i, 0))
    consts, const_specs = _split_consts(list(weights) + [perms[d] for d in sorted(set(dils) | {chunk})])
    in_specs = [row(D_MODEL)] + const_specs + [kept(n_prev)] * (2 if n_prev else 0)
    out_specs = [row(D_MODEL), row(D_ATT), row(D_ATT), row(D_ATT)]
    out_shape = [jax.ShapeDtypeStruct((m, D_MODEL), F32)] + [jax.ShapeDtypeStruct((m, D_ATT), BF16)] * 3
    for d in dils:
        out_specs += [pl.BlockSpec((1, d, tm // d, D_ATT), lambda i: (i // per_seq, 0, i % per_seq, 0))] * 3
        out_shape += [jax.ShapeDtypeStruct((m // seq, d, seq // d, D_ATT), BF16)] * 3
    out_specs += [kept(n_prev + 1), kept(n_prev + 1), row(D_SSM),
                  pl.BlockSpec((N_JT, tm // chunk, chunk * LANE), lambda i: (0, i, 0))]
    out_shape += [kept_shape, kept_shape, jax.ShapeDtypeStruct((m, D_SSM), F32),
                  jax.ShapeDtypeStruct((N_JT, m // chunk, chunk * LANE), BF16)]
    args, window_cfg = [x] + consts + (list(prev_state) if n_prev else []), None
    if window is not None:
        old, new, t_new = window
        depth, n_seq, length = old.shape[0], old.shape[1], old.shape[-1]
        n_parts = (m // tm) // (depth * n_seq)
        width = length // n_parts
        assert n_parts * depth * n_seq == m // tm and width * n_parts == length and width % LANE == 0
        assert new.shape[-1] == n_seq * t_new and new.shape[-1] % LANE == 0 and t_new <= LANE
        tiles = width // LANE
        where = lambda i: (i // (n_parts * n_seq), (i // n_parts) % n_seq, i % n_parts)
        part = pl.BlockSpec((1, 1, N_HEADS, HEAD_DIM, width), lambda i: (*where(i)[:2], 0, 0, where(i)[2]))
        after = pl.BlockSpec((1, 1, N_HEADS, HEAD_DIM, LANE),
                             lambda i: (*where(i)[:2], 0, 0,
                                        jnp.minimum((where(i)[2] + 1) * tiles, length // LANE - 1)))
        args += [old, old, new]
        in_specs += [part, after, pl.BlockSpec((1,) + new.shape[1:], lambda i: (where(i)[0], 0, 0, 0))]
        out_specs.append(part)
        out_shape.append(jax.ShapeDtypeStruct(old.shape, old.dtype))
        window_cfg = (n_parts, n_seq, t_new)
    return pl.pallas_call(
        functools.partial(_premix_kernel, dils=tuple(dils), chunk=chunk, per_seq=per_seq, skip=skip,
                          window=window_cfg, n_prev=n_prev),
        grid=(m // tm,),
        in_specs=in_specs, out_specs=out_specs, out_shape=out_shape,
        compiler_params=_params(1),
        name="premix",
    )(*args)


def _attn_prompt_kernel(q_ref, kc_ref, kp_ref, vc_ref, vp_ref, bias_ref, o_ref, l_ref, kbuf, vbuf, tab_ref,
                        *, qb):
    nk = 2 * N_STEP

    @pl.when((pl.program_id(0) == 0) & (pl.program_id(1) == 0) & (pl.program_id(2) == 0))
    def _():
        no_prev = lax.broadcasted_iota(jnp.int32, (N_STEP, nk), 1) < N_STEP
        for h in range(N_HEADS):
            row = jnp.broadcast_to(bias_ref[h:h + 1, :], (N_STEP, nk))
            tab = pltpu.roll(row, 0, 1, stride=1, stride_axis=0)
            tab_ref[0, h] = tab
            tab_ref[1, h] = jnp.where(no_prev, NEG, tab)

    first = pl.program_id(2) == 0
    kbuf[0:N_STEP, :] = kp_ref[...]
    kbuf[N_STEP:, :] = kc_ref[...]
    vbuf[0:N_STEP, :] = vp_ref[...]
    vbuf[N_STEP:, :] = vc_ref[...]
    lo_half = lax.broadcasted_iota(jnp.int32, (N_STEP, LANE), 1) < HEAD_DIM
    keep_lo = jnp.where(lo_half, 1.0, 0.0).astype(BF16)
    keep_hi = jnp.where(lo_half, 0.0, 1.0).astype(BF16)

    def block(qi, carry):
        r0 = pl.multiple_of(qi * N_STEP, N_STEP)
        which = jnp.where(first & (qi == 0), 1, 0)
        for hp in range(N_HEADS // 2):
            lanes = slice(hp * LANE, (hp + 1) * LANE)
            qp = q_ref[pl.ds(r0, N_STEP), lanes]
            kp = kbuf[pl.ds(r0, nk), lanes]
            vp = vbuf[pl.ds(r0, nk), lanes]
            o_pair, l_pair = None, None
            for hh in range(2):
                qm = qp * (keep_lo if hh == 0 else keep_hi)
                s = lax.dot_general(qm, kp, (((1,), (1,)), ((), ())), preferred_element_type=F32)
                s = s + tab_ref[which, 2 * hp + hh]
                m = jnp.max(s, axis=-1, keepdims=True)
                e = jnp.exp(s - m)
                den = jnp.sum(e, axis=-1, keepdims=True)
                o = _dot(e.astype(BF16), vp) / den
                lse = jnp.broadcast_to(m + jnp.log(den), (N_STEP, LANE))
                if hh == 0:
                    o_pair, l_pair = o, lse
                else:
                    o_pair = jnp.where(lo_half, o_pair, o)
                    l_pair = jnp.where(lo_half, l_pair, lse)
            o_ref[pl.ds(r0, N_STEP), lanes] = o_pair
            l_ref[pl.ds(r0, N_STEP), lanes] = l_pair
        return carry

    lax.fori_loop(0, qb // N_STEP, block, 0)


def _attn_prompt_branch(q, k, v, bias_row):
    batch, dil, sub, _ = q.shape
    qb = min(sub, 512)
    assert sub % qb == 0 and qb % N_STEP == 0
    per = qb // N_STEP
    sq = pl.Squeezed()
    cur = pl.BlockSpec((sq, sq, qb, D_ATT), lambda b, r, i: (b, r, i, 0))
    prev = pl.BlockSpec((sq, sq, N_STEP, D_ATT), lambda b, r, i: (b, r, jnp.maximum(i * per - 1, 0), 0))
    return pl.pallas_call(
        functools.partial(_attn_prompt_kernel, qb=qb),
        grid=(batch, dil, sub // qb),
        in_specs=[cur, cur, prev, cur, prev, _const_spec(bias_row.shape)],
        out_specs=[cur, cur],
        out_shape=[jax.ShapeDtypeStruct(q.shape, F32)] * 2,
        scratch_shapes=[pltpu.VMEM((qb + N_STEP, D_ATT), BF16)] * 2
                       + [pltpu.VMEM((2, N_HEADS, N_STEP, 2 * N_STEP), F32)],
        compiler_params=_params(3),
        name=f"attn_prompt_d{dil}",
    )(q, k, k, v, v, bias_row)


Q_PAD = 8


def _sample_steps(cache_len, t_new, n_new):
    step_of = np.full((len(BRANCHES), t_new, cache_len + n_new), -1, np.int32)
    for bi, (w, d) in enumerate(BRANCHES):
        for i in range(t_new):
            for j in range(w // d + 1):
                ext = cache_len + i - j * d
                assert ext >= 0
                step_of[bi, i, ext] = j
    return step_of


def _attn_sample_kernel(q_ref, kn_ref, vn_ref, kc_ref, vc_ref, tab_ref, o_ref, qs, *, t_new):
    n_new = kn_ref.shape[2]
    to_front = (n_new - pl.program_id(0) * t_new) % n_new
    q = q_ref[0].astype(F32)
    qs[...] = jnp.zeros(qs.shape, F32)
    for h in range(N_HEADS):
        qs[h, 0:t_new, :] = q[:, h * HEAD_DIM:(h + 1) * HEAD_DIM]
        qh = qs[h].astype(BF16)
        kn = pltpu.roll(kn_ref[h], to_front, 1).astype(BF16)
        vn = pltpu.roll(vn_ref[h], to_front, 1).astype(BF16)
        s = jnp.concatenate([_dot(qh, kc_ref[0, h].astype(BF16)), _dot(qh, kn)], axis=-1)
        logits = [s + tab_ref[bi, h] for bi in range(len(BRANCHES))]
        m = jnp.max(functools.reduce(jnp.maximum, logits), axis=-1, keepdims=True)
        e = sum(jnp.exp(l - m) for l in logits)
        den = jnp.sum(e, axis=-1, keepdims=True)
        eb = e.astype(BF16)
        nt = (((1,), (1,)), ((), ()))
        o = (lax.dot_general(eb[:, :-n_new], vc_ref[0, h].astype(BF16), nt, preferred_element_type=F32)
             + lax.dot_general(eb[:, -n_new:], vn, nt, preferred_element_type=F32)) / den
        o_ref[0, :, h * HEAD_DIM:(h + 1) * HEAD_DIM] = o[:t_new]


def _attn_sample(q, k_new, v_new, cache_k, cache_v, layer, tab, batch, t_new):
    cache_len = cache_k.shape[-1]
    tok = pl.BlockSpec((1, t_new, D_ATT), lambda b: (b, 0, 0))
    slab = pl.BlockSpec((pl.Squeezed(), 1, N_HEADS, HEAD_DIM, cache_len), lambda b: (layer, b, 0, 0, 0))
    out = pl.pallas_call(
        functools.partial(_attn_sample_kernel, t_new=t_new),
        grid=(batch,),
        in_specs=[tok, _const_spec(k_new.shape), _const_spec(v_new.shape), slab, slab, _const_spec(tab.shape)],
        out_specs=tok,
        out_shape=jax.ShapeDtypeStruct((batch, t_new, D_ATT), F32),
        scratch_shapes=[pltpu.VMEM((N_HEADS, Q_PAD, HEAD_DIM), F32)],
        compiler_params=_params(1),
        name="attn_sample",
    )(q.reshape(batch, t_new, D_ATT), k_new, v_new, cache_k, cache_v, tab)
    return out.reshape(batch * t_new, D_ATT)


def _ssm_kernel(u_ref, h0_ref, w_ref, pw_ref, qw_ref, a_ref, y_ref, hl_ref, z_buf, hb_buf,
                *, chunk, scan_rows):
    n_pair = chunk // TOK_PER_PAIR
    ub = u_ref[0]
    z = _dot(ub, pw_ref[0])
    gr, gi = a_ref[0, 0], a_ref[0, 1]
    h0 = h0_ref[0, 0]

    def times(m, xr, xi):
        cr, ci = gr[m:m + 1], gi[m:m + 1]
        return cr * xr - ci * xi, cr * xi + ci * xr

    if scan_rows:
        rows = ub.shape[0]
        in_group = lax.broadcasted_iota(jnp.int32, (rows, PLANE), 0) % SUB

        def below(x, sh):
            return jnp.where(in_group >= sh, pltpu.roll(x, sh, 0), 0.0)

        lr, li = z[:, 0:PLANE], z[:, PLANE:]
        sh = 1
        while sh < SUB:
            pr, pi = times(sh, below(lr, sh), below(li, sh))
            lr, li = lr + pr, li + pi
            sh *= 2
        z_buf[:, 0:PLANE] = lr
        z_buf[:, PLANE:] = li

        def group(b, h):
            hr, hi = h
            r0 = pl.multiple_of(b * SUB, SUB)
            hb_buf[pl.ds(r0, SUB), 0:PLANE] = jnp.broadcast_to(hr, (SUB, PLANE))
            hb_buf[pl.ds(r0, SUB), PLANE:] = jnp.broadcast_to(hi, (SUB, PLANE))
            pr, pi = times(SUB, hr, hi)
            return (pr + z_buf[pl.ds(r0 + SUB - 1, 1), 0:PLANE], pi + z_buf[pl.ds(r0 + SUB - 1, 1), PLANE:])

        hr, hi = lax.fori_loop(0, rows // SUB, group, (h0[:, 0:PLANE], h0[:, PLANE:]))
        hl_ref[0, 0, :, 0:PLANE] = hr
        hl_ref[0, 0, :, PLANE:] = hi
        pw_r = jnp.broadcast_to(gr[0:SUB], (rows // SUB, SUB, PLANE)).reshape(rows, PLANE)
        pw_i = jnp.broadcast_to(gi[0:SUB], (rows // SUB, SUB, PLANE)).reshape(rows, PLANE)
        sr, si = hb_buf[:, 0:PLANE], hb_buf[:, PLANE:]
        hb = jnp.concatenate([pw_r * sr - pw_i * si + below(lr, 1), pw_r * si + pw_i * sr + below(li, 1)],
                             axis=-1).astype(BF16)
    else:
        hr, hi = h0[:, 0:PLANE], h0[:, PLANE:]
        pr, pi = times(1, hr, hi)
        hl_ref[0, 0, :, 0:PLANE] = pr + z[:, 0:PLANE]
        hl_ref[0, 0, :, PLANE:] = pi + z[:, PLANE:]
        hb = h0.astype(BF16)
    for b in range(n_pair):
        acc = _dot(hb, qw_ref[0, b])
        for a in range(b + 1):
            acc = acc + _dot(ub[:, a * MXU:(a + 1) * MXU], w_ref[0, b - a])
        y_ref[0, :, b * MXU:(b + 1) * MXU] = acc


def _ssm(uc, h0, tabs, chunk, rows_per_block, scan_rows):
    w, pw, qw, a_pow = tabs
    rows, width = uc.shape[1:]
    n_blocks = rows // rows_per_block
    seqs = h0.shape[2]
    return pl.pallas_call(
        functools.partial(_ssm_kernel, chunk=chunk, scan_rows=scan_rows),
        grid=(N_JT, n_blocks),
        in_specs=[pl.BlockSpec((1, rows_per_block, width), lambda j, b: (j, b, 0)),
                  pl.BlockSpec((1, 1, seqs, 2 * PLANE), lambda j, b: (j, b, 0, 0)),
                  pl.BlockSpec((1,) + w.shape[1:], lambda j, b: (j, 0, 0, 0)),
                  pl.BlockSpec((1,) + pw.shape[1:], lambda j, b: (j, 0, 0)),
                  pl.BlockSpec((1,) + qw.shape[1:], lambda j, b: (j, 0, 0, 0)),
                  pl.BlockSpec((1,) + a_pow.shape[1:], lambda j, b: (j, 0, 0, 0))],
        out_specs=[pl.BlockSpec((1, rows_per_block, width), lambda j, b: (j, b, 0)),
                   pl.BlockSpec((1, 1, seqs, 2 * PLANE), lambda j, b: (j, b, 0, 0))],
        out_shape=[jax.ShapeDtypeStruct(uc.shape, F32),
                   jax.ShapeDtypeStruct(h0.shape, F32)],
        scratch_shapes=[pltpu.VMEM((rows_per_block, 2 * PLANE), F32)] * 2,
        compiler_params=_params(2),
        name=f"ssm_chunk{chunk}",
    )(uc, h0, w, pw, qw, a_pow)


def _ssm_tables(a_re, a_im, log_dt, b_re, b_im, c_re, c_im, chunk):
    hi = lax.Precision.HIGHEST
    dt = jnp.exp(log_dt)[:, None]
    den = a_re * a_re + a_im * a_im
    er, ang = jnp.exp(a_re * dt), a_im * dt
    ab_re, ab_im = er * jnp.cos(ang), er * jnp.sin(ang)
    nr, ni = ab_re - 1.0, ab_im
    f_re, f_im = (nr * a_re + ni * a_im) / den, (ni * a_re - nr * a_im) / den
    bb_re = f_re[..., None] * b_re - f_im[..., None] * b_im
    bb_im = f_re[..., None] * b_im + f_im[..., None] * b_re

    def power(taus):
        taus = jnp.asarray(taus, F32)[:, None, None]
        mag, ang_t = jnp.exp(a_re * dt * taus), a_im * dt * taus
        return mag * jnp.cos(ang_t), mag * jnp.sin(ang_t)

    p_re, p_im = power(np.arange(chunk + 1))
    pr, pi = (x[..., None] for x in power(chunk - 1 - np.arange(chunk)))
    s_re, s_im = pr * bb_re - pi * bb_im, pr * bb_im + pi * bb_re
    qr, qi = p_re[1:, :, None, :], p_im[1:, :, None, :]
    e_re, e_im = c_re * qr - c_im * qi, c_re * qi + c_im * qr
    lr, li = p_re[:chunk, :, None, :], p_im[:chunk, :, None, :]
    cl_re, cl_im = c_re * lr - c_im * li, c_re * li + c_im * lr
    kern = jnp.einsum("tgon,gni->tgoi", jnp.concatenate([cl_re, -cl_im], axis=-1),
                      jnp.concatenate([bb_re, bb_im], axis=1), precision=hi)

    n_pair = chunk // TOK_PER_PAIR
    tile = lambda x: x.reshape((x.shape[0], N_JT, GROUPS_PER_TILE) + x.shape[2:])
    gi = GROUPS_PER_TILE
    mask_w = jnp.asarray((np.arange(LANE) // SSM_GROUP)[:, None] == (np.arange(LANE) // SSM_GROUP)[None], F32)
    blk = tile(kern).transpose(1, 0, 2, 4, 3).reshape(N_JT, chunk, LANE, SSM_GROUP)
    blk = jnp.tile(blk, (1, 1, 1, gi)) * mask_w
    lag_block = lambda lag: blk[:, lag] if lag >= 0 else jnp.zeros_like(blk[:, 0])
    w = jnp.stack([jnp.concatenate([jnp.concatenate([lag_block(TOK_PER_PAIR * d + t_out - t_in)
                                                      for t_out in range(TOK_PER_PAIR)], axis=-1)
                                    for t_in in range(TOK_PER_PAIR)], axis=-2)
                   for d in range(n_pair)], axis=1)
    row_g =(np.arange(chunk * LANE) // SSM_GROUP) % gi
    mask_p = jnp.asarray(row_g[:, None] == (np.arange(PLANE) // SSM_STATE)[None], F32)

    def state_image(s):
        a = tile(s).transpose(1, 0, 2, 4, 3).reshape(N_JT, chunk * LANE, SSM_STATE)
        return jnp.tile(a, (1, 1, gi)) * mask_p

    pw = jnp.concatenate([state_image(s_re), state_image(s_im)], axis=-1)
    mask_q = jnp.asarray((np.arange(PLANE) // SSM_STATE)[:, None] == (np.arange(LANE) // SSM_GROUP)[None], F32)

    def readout(e):
        a = tile(e).transpose(1, 0, 2, 4, 3)
        a = a.reshape(N_JT, n_pair, TOK_PER_PAIR, gi, SSM_STATE, SSM_GROUP).transpose(0, 1, 3, 4, 2, 5)
        a = a.reshape(N_JT, n_pair, PLANE, TOK_PER_PAIR, SSM_GROUP)
        return jnp.concatenate([jnp.tile(a[:, :, :, t], (1, 1, 1, gi)) * mask_q
                                for t in range(TOK_PER_PAIR)], axis=-1)

    qw = jnp.concatenate([readout(e_re), readout(-e_im)], axis=2)
    a_pow = jnp.stack([x.reshape(SUB + 1, N_JT, PLANE).transpose(1, 0, 2)
                       for x in power(chunk * np.arange(SUB + 1))], axis=1)
    return w.astype(BF16), pw.astype(BF16), qw.astype(BF16), a_pow


def _state_to_tiles(re, im, n_blocks):
    n_seq = re.shape[0]
    t = lambda x: x.reshape(n_blocks, n_seq // n_blocks, N_JT, PLANE).transpose(2, 0, 1, 3)
    return jnp.concatenate([t(re), t(im)], axis=-1)


def _tiles_to_state(h):
    n_seq = h.shape[1] * h.shape[2]
    t = lambda x: x.transpose(1, 2, 0, 3).reshape(n_seq, N_GROUPS, SSM_STATE)
    return t(h[..., :PLANE]), t(h[..., PLANE:])


def _postmix_kernel(*refs, dils, chunk, final):
    it = iter(refs)
    x1_ref = next(it)
    att_refs = [(next(it), next(it)) for _ in dils] if len(dils) > 1 else [(next(it), None)]
    y_ref, uf_ref, p_ref = next(it), next(it), next(it)
    (natt_ref, nssm_ref, dskip_ref, wglu_ref, bglu_ref, wout_ref, nf_ref, wg_ref, wu_ref, wd_ref,
     nple_ref, wpg_ref, wpp_ref) = (next(it) for _ in range(13))
    perm_t = {d: next(it) for d in sorted((set(dils) | {chunk}) - {1})}
    nfin_ref = next(it) if final else None
    out_ref, ybuf = next(it), next(it)
    tm = x1_ref.shape[0]

    def natural(ref, d):
        val = ref[...] if d == 1 else ref[0].reshape(tm, D_ATT)
        return val if d == 1 else _permute_f32(perm_t[d], val)

    if len(dils) == 1:
        att = att_refs[0][0][...]
    else:
        outs = [natural(o, d) for d, (o, _) in zip(dils, att_refs)]
        lses = [natural(l, d) for d, (_, l) in zip(dils, att_refs)]
        top = functools.reduce(jnp.maximum, lses)
        ws = [jnp.exp(l - top) for l in lses]
        att = sum(w * o for w, o in zip(ws, outs)) / sum(ws)
    a_n = _rms(att, natt_ref[...])
    rows = tm // chunk
    for t in range(chunk):
        for j in range(N_JT):
            ybuf[t * rows:(t + 1) * rows, j * LANE:(j + 1) * LANE] = y_ref[j, :, t * LANE:(t + 1) * LANE]
    y = _permute_f32(perm_t[chunk], ybuf[...]) + dskip_ref[...] * uf_ref[...]
    z = jax.nn.gelu(y)
    s = z * jax.nn.sigmoid(_dot(z.astype(BF16), wglu_ref[...]) + bglu_ref[...])
    s_n = _rms(s, nssm_ref[...])
    mixed = jnp.concatenate([a_n, s_n], axis=-1).astype(BF16)
    x2 = x1_ref[...] + _dot(mixed, wout_ref[...])
    x3 = _ffn(x2, nf_ref[...], wg_ref, wu_ref, wd_ref)
    gate = jax.nn.sigmoid(_dot(_rms(x3, nple_ref[...]).astype(BF16), wpg_ref[...]))
    x4 = x3 + gate * _dot(p_ref[...].astype(BF16), wpp_ref[...])
    out_ref[...] = _rms(x4, nfin_ref[...]) if final else x4


def _postmix(x1, att_planes, dils, y, uf, p, layer, weights, perms_t, nfin, tm, seq, chunk):
    m = x1.shape[0]
    per_seq = seq // tm
    final = nfin is not None
    row = lambda w: pl.BlockSpec((tm, w), lambda i: (i, 0))
    att_args, att_specs = [], []
    for d, (o, l) in zip(dils, att_planes):
        spec = row(D_ATT) if d == 1 else pl.BlockSpec((1, d, tm // d, D_ATT),
                                                       lambda i: (i // per_seq, 0, i % per_seq, 0))
        for a in ((o, l) if len(dils) > 1 else (o,)):
            att_args.append(a.reshape(m, D_ATT) if d == 1 else a)
            att_specs.append(spec)
    consts, const_specs = _split_consts(list(weights) + [perms_t[d] for d in sorted((set(dils) | {chunk}) - {1})]
                                        + ([nfin] if final else []))
    args = [x1] + att_args + [y, uf, p] + consts
    in_specs = ([row(D_MODEL)] + att_specs
                + [pl.BlockSpec((N_JT, tm // chunk, chunk * LANE), lambda i: (0, i, 0)), row(D_SSM),
                   pl.BlockSpec((pl.Squeezed(), tm, D_PLE), lambda i: (layer, i, 0))]
                + const_specs)
    return pl.pallas_call(
        functools.partial(_postmix_kernel, dils=tuple(dils), chunk=chunk, final=final),
        grid=(m // tm,),
        in_specs=in_specs,
        out_specs=row(D_MODEL),
        out_shape=jax.ShapeDtypeStruct((m, D_MODEL), F32),
        scratch_shapes=[pltpu.VMEM((tm, D_SSM), F32)],
        compiler_params=_params(1),
        name="postmix",
    )(*args)


def _t5_bucket(dist):
    dist = np.asarray(dist, dtype=np.int64)
    exact = N_BUCKETS // 2
    ratio = np.log(np.maximum(dist, 1) / exact) / np.log(MAX_DISTANCE / exact)
    large = np.minimum(exact + (ratio * (N_BUCKETS - exact)).astype(np.int64), N_BUCKETS - 1)
    return np.where(dist < exact, dist, large).astype(np.int32)


def _branch_bias(rel_bias, dil):
    return rel_bias[_t5_bucket(np.arange(N_STEP + 1) * dil)].T.astype(F32)


def _prompt_bias_row(rel_bias, dil):
    step = N_STEP - np.arange(2 * N_STEP)
    row = rel_bias[_t5_bucket(np.maximum(step, 0) * dil)].T.astype(F32)
    return jnp.where((step >= 0)[None], row, NEG)


def _sample_table(biases, step_of):
    n_br, t_new, cols = step_of.shape
    tabs = []
    for bi in range(n_br):
        vals = biases[bi][:, np.maximum(step_of[bi], 0)]
        tab = jnp.where((step_of[bi] >= 0)[None], vals, NEG)
        tabs.append(jnp.concatenate([tab, jnp.zeros((N_HEADS, Q_PAD - t_new, cols), F32)], axis=1))
    return jnp.stack(tabs)


def kernel(x_prompt, x_sample, p_prompt, p_sample, cache_k, cache_v, state_ssm_re, state_ssm_im,
           rel_bias, w_in, w_out, norm_mix, norm_att_out, norm_ssm_out, norm_ffn, ffn_w_gate,
           ffn_w_up, ffn_w_down, ssm_a_re, ssm_a_im, ssm_log_dt, ssm_b_re, ssm_b_im, ssm_c_re,
           ssm_c_im, ssm_d, w_glu, b_glu, norm_ple, w_ple_gate, w_ple_proj, norm_final):
    depth = w_in.shape[0]
    batch, seq, _ = x_prompt.shape
    dec_batch, dec_seq, _ = x_sample.shape
    cache_len = cache_k.shape[2]
    keep = min(BRANCHES[-1][0], seq)
    dils = tuple(d for _, d in BRANCHES)
    chunk_p = dils[-1]
    tm_p = 256
    tm_s = dec_batch * dec_seq
    assert seq % tm_p == 0 and chunk_p in dils and dec_seq % TOK_PER_PAIR == 0 and tm_s % 8 == 0

    xp = x_prompt.reshape(batch * seq, D_MODEL)
    xs = x_sample.reshape(tm_s, D_MODEL)
    biases = [_branch_bias(rel_bias, d) for d in dils]
    tabs_p = [_prompt_bias_row(rel_bias, d) for d in dils]
    tab_s = _sample_table(biases, _sample_steps(cache_len, dec_seq, tm_s))
    to_lanes = lambda a: a.transpose(0, 1, 3, 4, 2)
    from_lanes = lambda a: a.transpose(0, 1, 4, 2, 3)
    cache_kt, cache_vt = to_lanes(cache_k), to_lanes(cache_v)
    perms_p = {d: jnp.asarray(_perm_matrix(tm_p, d), BF16) for d in dils if d > 1}
    perms_pt = {d: jnp.asarray(_perm_matrix(tm_p, d).T, BF16) for d in dils if d > 1}
    perms_s = {dec_seq: jnp.asarray(_perm_matrix(tm_s, dec_seq), BF16)}
    perms_st = {dec_seq: jnp.asarray(_perm_matrix(tm_s, dec_seq).T, BF16)}
    row = lambda v: v.reshape(1, -1)
    bf = lambda a: a.astype(BF16)

    wg, wu, wd = bf(ffn_w_gate), bf(ffn_w_up), bf(ffn_w_down)
    win, wout, wglu, wpg, wpp = bf(w_in), bf(w_out), bf(w_glu), bf(w_ple_gate), bf(w_ple_proj)

    def layer_weights(i):
        pre = (row(norm_ffn[i, 0]), (wg, (i, 0)), (wu, (i, 0)), (wd, (i, 0)), row(norm_mix[i]), (win, (i,)))
        post = (row(norm_att_out[i]), row(norm_ssm_out[i]), row(ssm_d[i]), (wglu, (i,)), row(b_glu[i]),
                (wout, (i,)), row(norm_ffn[i, 1]), (wg, (i, 1)), (wu, (i, 1)), (wd, (i, 1)),
                row(norm_ple[i]), (wpg, (i,)), (wpp, (i,)))
        nfin = row(norm_final) if i == depth - 1 else None
        ssm_par = (ssm_a_re[i], ssm_a_im[i], ssm_log_dt[i], ssm_b_re[i], ssm_b_im[i], ssm_c_re[i], ssm_c_im[i])
        return pre, post, nfin, ssm_par

    k_news, v_news, rss, iss = [], [], [], []
    for i in range(depth):
        pre, post, nfin, ssm_par = layer_weights(i)
        x1, q, _, _, kf, vf, uf, uc = _premix(xs, pre, perms_s, tm_s, tm_s, tm_s, (), dec_seq)
        att = _attn_sample(q, kf[0, 0], vf[0, 0], cache_kt, cache_vt, i, tab_s, dec_batch, dec_seq)
        h0 = _state_to_tiles(state_ssm_re[i], state_ssm_im[i], 1)
        y, h_last = _ssm(uc, h0, _ssm_tables(*ssm_par, dec_seq), dec_seq, dec_batch, False)
        xs = _postmix(x1, [(att, None)], (1,), y, uf, p_sample.reshape(depth, tm_s, D_PLE), i, post, perms_st,
                      nfin, tm_s, tm_s, dec_seq)
        k_news.append(kf[0, 0])
        v_news.append(vf[0, 0])
        re, im = _tiles_to_state(h_last)
        rss.append(re)
        iss.append(im)

    assert depth >= 2
    windows = [(cache_kt, jnp.stack(k_news), dec_seq), (cache_vt, jnp.stack(v_news), dec_seq)]
    kv_state, rp, ip = None, [], []
    for i in range(depth):
        pre, post, nfin, ssm_par = layer_weights(i)
        outs = list(_premix(xp, pre, perms_p, tm_p, seq, keep, dils[1:], chunk_p,
                            window=windows[i] if i < 2 else None, prev_state=kv_state))
        if i < 2:
            windows[i] = outs.pop()
        x1, q, k, v = outs[:4]
        kf, vf, uf, uc = outs[-4:]
        kv_state = (kf, vf)
        nat = lambda a: a.reshape(batch, 1, seq, D_ATT)
        qkv = [(nat(q), nat(k), nat(v))] + [tuple(outs[4 + 3 * n:7 + 3 * n]) for n in range(len(dils) - 1)]
        planes = [_attn_prompt_branch(*qkv[bi], tabs_p[bi]) for bi in range(len(dils))]
        zeros = jnp.zeros((batch, N_GROUPS, SSM_STATE), F32)
        y, h_last = _ssm(uc, _state_to_tiles(zeros, zeros, batch), _ssm_tables(*ssm_par, chunk_p),
                         chunk_p, seq // chunk_p, True)
        xp = _postmix(x1, planes, dils, y, uf, p_prompt.reshape(depth, batch * seq, D_PLE), i, post, perms_pt,
                      nfin, tm_p, seq, chunk_p)
        re, im = _tiles_to_state(h_last)
        rp.append(re)
        ip.append(im)

    return (xp.reshape(batch, seq, D_MODEL), xs.reshape(dec_batch, dec_seq, D_MODEL),
            from_lanes(kv_state[0]), from_lanes(kv_state[1]), jnp.stack(rp), jnp.stack(ip),
            from_lanes(windows[0]), from_lanes(windows[1]), jnp.stack(rss), jnp.stack(iss))
```

```python
import functools
import math

import numpy as np
import jax
import jax.numpy as jnp
from jax import lax
from jax.experimental import pallas as pl
from jax.experimental.pallas import tpu as pltpu

F32 = jnp.float32
BF16 = jnp.bfloat16

D_MODEL = 1024
D_ATT = 512
D_SSM = 512
HEAD_DIM = 64
N_HEADS = 8
SSM_GROUP = 16
N_GROUPS = 32
SSM_STATE = 64
BRANCHES = ((128, 1), (512, 4), (2048, 16))
N_STEP = 128
N_BUCKETS = 32
MAX_DISTANCE = 2048
D_PLE = 256
EPS = 1e-6
NEG = -1e30

LANE = 128
SUB = 8
MXU = 256
GROUPS_PER_TILE = LANE // SSM_GROUP
N_JT = D_SSM // LANE
PLANE = GROUPS_PER_TILE * SSM_STATE
TOK_PER_PAIR = MXU // LANE
VMEM_LIMIT = 56 * 1024 * 1024

assert all(w // d == N_STEP for w, d in BRANCHES)


def _rms(x, g):
    return x * lax.rsqrt(jnp.mean(x * x, axis=-1, keepdims=True) + EPS) * g


def _dot(a, b):
    return jnp.dot(a, b, preferred_element_type=F32)


def _ffn(x, g, wg_ref, wu_ref, wd_ref):
    h = _rms(x, g).astype(BF16)
    act = (jax.nn.silu(_dot(h, wg_ref[...])) * _dot(h, wu_ref[...])).astype(BF16)
    return x + 0.5 * _dot(act, wd_ref[...])


def _perm_matrix(n, d):
    p = np.zeros((n, n), np.float32)
    s, r = np.meshgrid(np.arange(n // d), np.arange(d), indexing="ij")
    p[(r * (n // d) + s).ravel(), (d * s + r).ravel()] = 1.0
    return p


def _permute_f32(pt_ref, x):
    hi = x.astype(BF16)
    lo = (x - hi.astype(F32)).astype(BF16)
    pt = pt_ref[...]
    return _dot(pt, hi) + _dot(pt, lo)


def _const_spec(shape, lead=()):
    tail = tuple(shape[len(lead):])
    return pl.BlockSpec((pl.Squeezed(),) * len(lead) + tail, lambda *_: tuple(lead) + (0,) * len(tail),
                        pipeline_mode=pl.Buffered(1))


def _split_consts(consts):
    pairs = [c if isinstance(c, tuple) else (c, ()) for c in consts]
    return [a for a, _ in pairs], [_const_spec(a.shape, lead) for a, lead in pairs]


def _params(n_axes):
    return pltpu.CompilerParams(dimension_semantics=("arbitrary",) * n_axes,
                                vmem_limit_bytes=VMEM_LIMIT)


def _window_part(old_ref, next_ref, new_ref, out_ref, part, n_parts, seq_idx, t_new):
    width, n_new = old_ref.shape[-1], new_ref.shape[-1]
    to_back = (LANE - t_new - seq_idx * t_new) % n_new
    tail = lax.broadcasted_iota(jnp.int32, (HEAD_DIM, LANE), 1) >= LANE - t_new
    for h in range(N_HEADS):
        shifted = pltpu.roll(old_ref[0, 0, h], width - t_new, 1)
        incoming = jnp.where(part == n_parts - 1,
                             pltpu.roll(new_ref[0, h], to_back, 1)[:, :LANE],
                             pltpu.roll(next_ref[0, 0, h], LANE - t_new, 1))
        out_ref[0, 0, h, :, :width - LANE] = shifted[:, :width - LANE]
        out_ref[0, 0, h, :, width - LANE:] = jnp.where(tail, incoming, shifted[:, width - LANE:])


def _premix_kernel(*refs, dils, chunk, per_seq, skip, window, n_prev):
    it = iter(refs)
    x_ref, nf_ref, wg_ref, wu_ref, wd_ref, nm_ref, win_ref = (next(it) for _ in range(7))
    perm = {d: next(it) for d in sorted(set(dils) | {chunk})}
    kprev_ref, vprev_ref = (next(it), next(it)) if n_prev else (None, None)
    window_in = [next(it) for _ in range(3)] if window else None
    x1_ref, q_ref, k_ref, v_ref = (next(it) for _ in range(4))
    strided = {d: (next(it), next(it), next(it)) for d in dils}
    kf_ref, vf_ref, uf_ref, uc_ref = (next(it) for _ in range(4))
    if window:
        n_parts, n_seq, t_new = window
        step = pl.program_id(0)
        _window_part(*window_in, next(it), step % n_parts, n_parts, (step // n_parts) % n_seq, t_new)

    x1 = _ffn(x_ref[...], nf_ref[...], wg_ref, wu_ref, wd_ref)
    x1_ref[...] = x1
    h = _rms(x1, nm_ref[...]).astype(BF16)
    qkvu = _dot(h, win_ref[...])
    q = (qkvu[:, :D_ATT] * (1.0 / math.sqrt(HEAD_DIM))).astype(BF16)
    kf = qkvu[:, D_ATT:2 * D_ATT]
    vf = qkvu[:, 2 * D_ATT:3 * D_ATT]
    uf = qkvu[:, 3 * D_ATT:]
    k, v, u = kf.astype(BF16), vf.astype(BF16), uf.astype(BF16)
    q_ref[...], k_ref[...], v_ref[...] = q, k, v
    uf_ref[...] = uf
    tm = x1.shape[0]

    @pl.when(pl.program_id(0) % per_seq >= skip)
    def _():
        for layer in range(n_prev):
            kf_ref[layer] = kprev_ref[layer]
            vf_ref[layer] = vprev_ref[layer]
        kf_ref[n_prev, 0] = kf.T.reshape(N_HEADS, HEAD_DIM, tm)
        vf_ref[n_prev, 0] = vf.T.reshape(N_HEADS, HEAD_DIM, tm)

    for d, outs in strided.items():
        for val, out in zip((q, k, v), outs):
            out[0] = _dot(perm[d][...], val).astype(BF16).reshape(d, tm // d, D_ATT)
    up = _dot(perm[chunk][...], u).astype(BF16)
    rows = tm // chunk
    for t in range(chunk):
        for j in range(N_JT):
            uc_ref[j, :, t * LANE:(t + 1) * LANE] = up[t * rows:(t + 1) * rows, j * LANE:(j + 1) * LANE]


def _premix(x, weights, perms, tm, seq, keep, dils, chunk, window=None, prev_state=None):
    m = x.shape[0]
    per_seq = seq // tm
    skip = (seq - keep) // tm
    assert keep % tm == 0 and tm % LANE == 0
    n_prev = 0 if prev_state is None else prev_state[0].shape[0]
    kept = lambda layers: pl.BlockSpec((layers, 1, N_HEADS, HEAD_DIM, tm),
                                       lambda i: (0, i // per_seq, 0, 0, jnp.maximum(i % per_seq - skip, 0)))
    kept_shape = jax.ShapeDtypeStruct((n_prev + 1, m // seq, N_HEADS, HEAD_DIM, keep), F32)
    row = lambda w: pl.BlockSpec((tm, w), lambda i: (i, 0))
    consts, const_specs = _split_consts(list(weights) + [perms[d] for d in sorted(set(dils) | {chunk})])
    in_specs = [row(D_MODEL)] + const_specs + [kept(n_prev)] * (2 if n_prev else 0)
    out_specs = [row(D_MODEL), row(D_ATT), row(D_ATT), row(D_ATT)]
    out_shape = [jax.ShapeDtypeStruct((m, D_MODEL), F32)] + [jax.ShapeDtypeStruct((m, D_ATT), BF16)] * 3
    for d in dils:
        out_specs += [pl.BlockSpec((1, d, tm // d, D_ATT), lambda i: (i // per_seq, 0, i % per_seq, 0))] * 3
        out_shape += [jax.ShapeDtypeStruct((m // seq, d, seq // d, D_ATT), BF16)] * 3
    out_specs += [kept(n_prev + 1), kept(n_prev + 1), row(D_SSM),
                  pl.BlockSpec((N_JT, tm // chunk, chunk * LANE), lambda i: (0, i, 0))]
    out_shape += [kept_shape, kept_shape, jax.ShapeDtypeStruct((m, D_SSM), F32),
                  jax.ShapeDtypeStruct((N_JT, m // chunk, chunk * LANE), BF16)]
    args, window_cfg = [x] + consts + (list(prev_state) if n_prev else []), None
    if window is not None:
        old, new, t_new = window
        depth, n_seq, length = old.shape[0], old.shape[1], old.shape[-1]
        n_parts = (m // tm) // (depth * n_seq)
        width = length // n_parts
        assert n_parts * depth * n_seq == m // tm and width * n_parts == length and width % LANE == 0
        assert new.shape[-1] == n_seq * t_new and new.shape[-1] % LANE == 0 and t_new <= LANE
        tiles = width // LANE
        where = lambda i: (i // (n_parts * n_seq), (i // n_parts) % n_seq, i % n_parts)
        part = pl.BlockSpec((1, 1, N_HEADS, HEAD_DIM, width), lambda i: (*where(i)[:2], 0, 0, where(i)[2]))
        after = pl.BlockSpec((1, 1, N_HEADS, HEAD_DIM, LANE),
                             lambda i: (*where(i)[:2], 0, 0,
                                        jnp.minimum((where(i)[2] + 1) * tiles, length // LANE - 1)))
        args += [old, old, new]
        in_specs += [part, after, pl.BlockSpec((1,) + new.shape[1:], lambda i: (where(i)[0], 0, 0, 0))]
        out_specs.append(part)
        out_shape.append(jax.ShapeDtypeStruct(old.shape, old.dtype))
        window_cfg = (n_parts, n_seq, t_new)
    return pl.pallas_call(
        functools.partial(_premix_kernel, dils=tuple(dils), chunk=chunk, per_seq=per_seq, skip=skip,
                          window=window_cfg, n_prev=n_prev),
        grid=(m // tm,),
        in_specs=in_specs, out_specs=out_specs, out_shape=out_shape,
        compiler_params=_params(1),
        name="premix",
    )(*args)


def _attn_prompt_kernel(q_ref, kc_ref, kp_ref, vc_ref, vp_ref, bias_ref, o_ref, l_ref, kbuf, vbuf, tab_ref,
                        *, qb):
    nk = 2 * N_STEP

    @pl.when((pl.program_id(0) == 0) & (pl.program_id(1) == 0) & (pl.program_id(2) == 0))
    def _():
        no_prev = lax.broadcasted_iota(jnp.int32, (N_STEP, nk), 1) < N_STEP
        for h in range(N_HEADS):
            row = jnp.broadcast_to(bias_ref[h:h + 1, :], (N_STEP, nk))
            tab = pltpu.roll(row, 0, 1, stride=1, stride_axis=0)
            tab_ref[0, h] = tab
            tab_ref[1, h] = jnp.where(no_prev, NEG, tab)

    first = pl.program_id(2) == 0
    kbuf[0:N_STEP, :] = kp_ref[...]
    kbuf[N_STEP:, :] = kc_ref[...]
    vbuf[0:N_STEP, :] = vp_ref[...]
    vbuf[N_STEP:, :] = vc_ref[...]
    lo_half = lax.broadcasted_iota(jnp.int32, (N_STEP, LANE), 1) < HEAD_DIM
    keep_lo = jnp.where(lo_half, 1.0, 0.0).astype(BF16)
    keep_hi = jnp.where(lo_half, 0.0, 1.0).astype(BF16)

    def block(qi, carry):
        r0 = pl.multiple_of(qi * N_STEP, N_STEP)
        which = jnp.where(first & (qi == 0), 1, 0)
        for hp in range(N_HEADS // 2):
            lanes = slice(hp * LANE, (hp + 1) * LANE)
            qp = q_ref[pl.ds(r0, N_STEP), lanes]
            kp = kbuf[pl.ds(r0, nk), lanes]
            vp = vbuf[pl.ds(r0, nk), lanes]
            o_pair, l_pair = None, None
            for hh in range(2):
                qm = qp * (keep_lo if hh == 0 else keep_hi)
                s = lax.dot_general(qm, kp, (((1,), (1,)), ((), ())), preferred_element_type=F32)
                s = s + tab_ref[which, 2 * hp + hh]
                m = jnp.max(s, axis=-1, keepdims=True)
                e = jnp.exp(s - m)
                den = jnp.sum(e, axis=-1, keepdims=True)
                o = _dot(e.astype(BF16), vp) / den
                lse = jnp.broadcast_to(m + jnp.log(den), (N_STEP, LANE))
                if hh == 0:
                    o_pair, l_pair = o, lse
                else:
                    o_pair = jnp.where(lo_half, o_pair, o)
                    l_pair = jnp.where(lo_half, l_pair, lse)
            o_ref[pl.ds(r0, N_STEP), lanes] = o_pair
            l_ref[pl.ds(r0, N_STEP), lanes] = l_pair
        return carry

    lax.fori_loop(0, qb // N_STEP, block, 0)


def _attn_prompt_branch(q, k, v, bias_row):
    batch, dil, sub, _ = q.shape
    qb = min(sub, 512)
    assert sub % qb == 0 and qb % N_STEP == 0
    per = qb // N_STEP
    sq = pl.Squeezed()
    cur = pl.BlockSpec((sq, sq, qb, D_ATT), lambda b, r, i: (b, r, i, 0))
    prev = pl.BlockSpec((sq, sq, N_STEP, D_ATT), lambda b, r, i: (b, r, jnp.maximum(i * per - 1, 0), 0))
    return pl.pallas_call(
        functools.partial(_attn_prompt_kernel, qb=qb),
        grid=(batch, dil, sub // qb),
        in_specs=[cur, cur, prev, cur, prev, _const_spec(bias_row.shape)],
        out_specs=[cur, cur],
        out_shape=[jax.ShapeDtypeStruct(q.shape, F32)] * 2,
        scratch_shapes=[pltpu.VMEM((qb + N_STEP, D_ATT), BF16)] * 2
                       + [pltpu.VMEM((2, N_HEADS, N_STEP, 2 * N_STEP), F32)],
        compiler_params=_params(3),
        name=f"attn_prompt_d{dil}",
    )(q, k, k, v, v, bias_row)


Q_PAD = 8


def _sample_steps(cache_len, t_new, n_new):
    step_of = np.full((len(BRANCHES), t_new, cache_len + n_new), -1, np.int32)
    for bi, (w, d) in enumerate(BRANCHES):
        for i in range(t_new):
            for j in range(w // d + 1):
                ext = cache_len + i - j * d
                assert ext >= 0
                step_of[bi, i, ext] = j
    return step_of


def _attn_sample_kernel(q_ref, kn_ref, vn_ref, kc_ref, vc_ref, tab_ref, o_ref, qs, *, t_new):
    n_new = kn_ref.shape[2]
    to_front = (n_new - pl.program_id(0) * t_new) % n_new
    q = q_ref[0].astype(F32)
    qs[...] = jnp.zeros(qs.shape, F32)
    for h in range(N_HEADS):
        qs[h, 0:t_new, :] = q[:, h * HEAD_DIM:(h + 1) * HEAD_DIM]
        qh = qs[h].astype(BF16)
        kn = pltpu.roll(kn_ref[h], to_front, 1).astype(BF16)
        vn = pltpu.roll(vn_ref[h], to_front, 1).astype(BF16)
        s = jnp.concatenate([_dot(qh, kc_ref[0, h].astype(BF16)), _dot(qh, kn)], axis=-1)
        logits = [s + tab_ref[bi, h] for bi in range(len(BRANCHES))]
        m = jnp.max(functools.reduce(jnp.maximum, logits), axis=-1, keepdims=True)
        e = sum(jnp.exp(l - m) for l in logits)
        den = jnp.sum(e, axis=-1, keepdims=True)
        eb = e.astype(BF16)
        nt = (((1,), (1,)), ((), ()))
        o = (lax.dot_general(eb[:, :-n_new], vc_ref[0, h].astype(BF16), nt, preferred_element_type=F32)
             + lax.dot_general(eb[:, -n_new:], vn, nt, preferred_element_type=F32)) / den
        o_ref[0, :, h * HEAD_DIM:(h + 1) * HEAD_DIM] = o[:t_new]


def _attn_sample(q, k_new, v_new, cache_k, cache_v, layer, tab, batch, t_new):
    cache_len = cache_k.shape[-1]
    tok = pl.BlockSpec((1, t_new, D_ATT), lambda b: (b, 0, 0))
    slab = pl.BlockSpec((pl.Squeezed(), 1, N_HEADS, HEAD_DIM, cache_len), lambda b: (layer, b, 0, 0, 0))
    out = pl.pallas_call(
        functools.partial(_attn_sample_kernel, t_new=t_new),
        grid=(batch,),
        in_specs=[tok, _const_spec(k_new.shape), _const_spec(v_new.shape), slab, slab, _const_spec(tab.shape)],
        out_specs=tok,
        out_shape=jax.ShapeDtypeStruct((batch, t_new, D_ATT), F32),
        scratch_shapes=[pltpu.VMEM((N_HEADS, Q_PAD, HEAD_DIM), F32)],
        compiler_params=_params(1),
        name="attn_sample",
    )(q.reshape(batch, t_new, D_ATT), k_new, v_new, cache_k, cache_v, tab)
    return out.reshape(batch * t_new, D_ATT)


def _ssm_kernel(u_ref, h0_ref, w_ref, pw_ref, qw_ref, a_ref, y_ref, hl_ref, z_buf, hb_buf,
                *, chunk, scan_rows):
    n_pair = chunk // TOK_PER_PAIR
    ub = u_ref[0]
    z = _dot(ub, pw_ref[0])
    gr, gi = a_ref[0, 0], a_ref[0, 1]
    h0 = h0_ref[0, 0]

    def times(m, xr, xi):
        cr, ci = gr[m:m + 1], gi[m:m + 1]
        return cr * xr - ci * xi, cr * xi + ci * xr

    if scan_rows:
        rows = ub.shape[0]
        in_group = lax.broadcasted_iota(jnp.int32, (rows, PLANE), 0) % SUB

        def below(x, sh):
            return jnp.where(in_group >= sh, pltpu.roll(x, sh, 0), 0.0)

        lr, li = z[:, 0:PLANE], z[:, PLANE:]
        sh = 1
        while sh < SUB:
            pr, pi = times(sh, below(lr, sh), below(li, sh))
            lr, li = lr + pr, li + pi
            sh *= 2
        z_buf[:, 0:PLANE] = lr
        z_buf[:, PLANE:] = li

        def group(b, h):
            hr, hi = h
            r0 = pl.multiple_of(b * SUB, SUB)
            hb_buf[pl.ds(r0, SUB), 0:PLANE] = jnp.broadcast_to(hr, (SUB, PLANE))
            hb_buf[pl.ds(r0, SUB), PLANE:] = jnp.broadcast_to(hi, (SUB, PLANE))
            pr, pi = times(SUB, hr, hi)
            return (pr + z_buf[pl.ds(r0 + SUB - 1, 1), 0:PLANE], pi + z_buf[pl.ds(r0 + SUB - 1, 1), PLANE:])

        hr, hi = lax.fori_loop(0, rows // SUB, group, (h0[:, 0:PLANE], h0[:, PLANE:]))
        hl_ref[0, 0, :, 0:PLANE] = hr
        hl_ref[0, 0, :, PLANE:] = hi
        pw_r = jnp.broadcast_to(gr[0:SUB], (rows // SUB, SUB, PLANE)).reshape(rows, PLANE)
        pw_i = jnp.broadcast_to(gi[0:SUB], (rows // SUB, SUB, PLANE)).reshape(rows, PLANE)
        sr, si = hb_buf[:, 0:PLANE], hb_buf[:, PLANE:]
        hb = jnp.concatenate([pw_r * sr - pw_i * si + below(lr, 1), pw_r * si + pw_i * sr + below(li, 1)],
                             axis=-1).astype(BF16)
    else:
        hr, hi = h0[:, 0:PLANE], h0[:, PLANE:]
        pr, pi = times(1, hr, hi)
        hl_ref[0, 0, :, 0:PLANE] = pr + z[:, 0:PLANE]
        hl_ref[0, 0, :, PLANE:] = pi + z[:, PLANE:]
        hb = h0.astype(BF16)
    for b in range(n_pair):
        acc = _dot(hb, qw_ref[0, b])
        for a in range(b + 1):
            acc = acc + _dot(ub[:, a * MXU:(a + 1) * MXU], w_ref[0, b - a])
        y_ref[0, :, b * MXU:(b + 1) * MXU] = acc


def _ssm(uc, h0, tabs, chunk, rows_per_block, scan_rows):
    w, pw, qw, a_pow = tabs
    rows, width = uc.shape[1:]
    n_blocks = rows // rows_per_block
    seqs = h0.shape[2]
    return pl.pallas_call(
        functools.partial(_ssm_kernel, chunk=chunk, scan_rows=scan_rows),
        grid=(N_JT, n_blocks),
        in_specs=[pl.BlockSpec((1, rows_per_block, width), lambda j, b: (j, b, 0)),
                  pl.BlockSpec((1, 1, seqs, 2 * PLANE), lambda j, b: (j, b, 0, 0)),
                  pl.BlockSpec((1,) + w.shape[1:], lambda j, b: (j, 0, 0, 0)),
                  pl.BlockSpec((1,) + pw.shape[1:], lambda j, b: (j, 0, 0)),
                  pl.BlockSpec((1,) + qw.shape[1:], lambda j, b: (j, 0, 0, 0)),
                  pl.BlockSpec((1,) + a_pow.shape[1:], lambda j, b: (j, 0, 0, 0))],
        out_specs=[pl.BlockSpec((1, rows_per_block, width), lambda j, b: (j, b, 0)),
                   pl.BlockSpec((1, 1, seqs, 2 * PLANE), lambda j, b: (j, b, 0, 0))],
        out_shape=[jax.ShapeDtypeStruct(uc.shape, F32),
                   jax.ShapeDtypeStruct(h0.shape, F32)],
        scratch_shapes=[pltpu.VMEM((rows_per_block, 2 * PLANE), F32)] * 2,
        compiler_params=_params(2),
        name=f"ssm_chunk{chunk}",
    )(uc, h0, w, pw, qw, a_pow)


def _ssm_tables(a_re, a_im, log_dt, b_re, b_im, c_re, c_im, chunk):
    hi = lax.Precision.HIGHEST
    dt = jnp.exp(log_dt)[:, None]
    den = a_re * a_re + a_im * a_im
    er, ang = jnp.exp(a_re * dt), a_im * dt
    ab_re, ab_im = er * jnp.cos(ang), er * jnp.sin(ang)
    nr, ni = ab_re - 1.0, ab_im
    f_re, f_im = (nr * a_re + ni * a_im) / den, (ni * a_re - nr * a_im) / den
    bb_re = f_re[..., None] * b_re - f_im[..., None] * b_im
    bb_im = f_re[..., None] * b_im + f_im[..., None] * b_re

    def power(taus):
        taus = jnp.asarray(taus, F32)[:, None, None]
        mag, ang_t = jnp.exp(a_re * dt * taus), a_im * dt * taus
        return mag * jnp.cos(ang_t), mag * jnp.sin(ang_t)

    p_re, p_im = power(np.arange(chunk + 1))
    pr, pi = (x[..., None] for x in power(chunk - 1 - np.arange(chunk)))
    s_re, s_im = pr * bb_re - pi * bb_im, pr * bb_im + pi * bb_re
    qr, qi = p_re[1:, :, None, :], p_im[1:, :, None, :]
    e_re, e_im = c_re * qr - c_im * qi, c_re * qi + c_im * qr
    lr, li = p_re[:chunk, :, None, :], p_im[:chunk, :, None, :]
    cl_re, cl_im = c_re * lr - c_im * li, c_re * li + c_im * lr
    kern = jnp.einsum("tgon,gni->tgoi", jnp.concatenate([cl_re, -cl_im], axis=-1),
                      jnp.concatenate([bb_re, bb_im], axis=1), precision=hi)

    n_pair = chunk // TOK_PER_PAIR
    tile = lambda x: x.reshape((x.shape[0], N_JT, GROUPS_PER_TILE) + x.shape[2:])
    gi = GROUPS_PER_TILE
    mask_w = jnp.asarray((np.arange(LANE) // SSM_GROUP)[:, None] == (np.arange(LANE) // SSM_GROUP)[None], F32)
    blk = tile(kern).transpose(1, 0, 2, 4, 3).reshape(N_JT, chunk, LANE, SSM_GROUP)
    blk = jnp.tile(blk, (1, 1, 1, gi)) * mask_w
    lag_block = lambda lag: blk[:, lag] if lag >= 0 else jnp.zeros_like(blk[:, 0])
    w = jnp.stack([jnp.concatenate([jnp.concatenate([lag_block(TOK_PER_PAIR * d + t_out - t_in)
                                                      for t_out in range(TOK_PER_PAIR)], axis=-1)
                                    for t_in in range(TOK_PER_PAIR)], axis=-2)
                   for d in range(n_pair)], axis=1)
    row_g = (np.arange(chunk * LANE) // SSM_GROUP) % gi
    mask_p = jnp.asarray(row_g[:, None] == (np.arange(PLANE) // SSM_STATE)[None], F32)

    def state_image(s):
        a = tile(s).transpose(1, 0, 2, 4, 3).reshape(N_JT, chunk * LANE, SSM_STATE)
        return jnp.tile(a, (1, 1, gi)) * mask_p

    pw = jnp.concatenate([state_image(s_re), state_image(s_im)], axis=-1)
    mask_q = jnp.asarray((np.arange(PLANE) // SSM_STATE)[:, None] == (np.arange(LANE) // SSM_GROUP)[None], F32)

    def readout(e):
        a = tile(e).transpose(1, 0, 2, 4, 3)
        a = a.reshape(N_JT, n_pair, TOK_PER_PAIR, gi, SSM_STATE, SSM_GROUP).transpose(0, 1, 3, 4, 2, 5)
        a = a.reshape(N_JT, n_pair, PLANE, TOK_PER_PAIR, SSM_GROUP)
        return jnp.concatenate([jnp.tile(a[:, :, :, t], (1, 1, 1, gi)) * mask_q
                                for t in range(TOK_PER_PAIR)], axis=-1)

    qw = jnp.concatenate([readout(e_re), readout(-e_im)], axis=2)
    a_pow = jnp.stack([x.reshape(SUB + 1, N_JT, PLANE).transpose(1, 0, 2)
                       for x in power(chunk * np.arange(SUB + 1))], axis=1)
    return w.astype(BF16), pw.astype(BF16), qw.astype(BF16), a_pow


def _state_to_tiles(re, im, n_blocks):
    n_seq = re.shape[0]
    t = lambda x: x.reshape(n_blocks, n_seq // n_blocks, N_JT, PLANE).transpose(2, 0, 1, 3)
    return jnp.concatenate([t(re), t(im)], axis=-1)


def _tiles_to_state(h):
    n_seq = h.shape[1] * h.shape[2]
    t = lambda x: x.transpose(1, 2, 0, 3).reshape(n_seq, N_GROUPS, SSM_STATE)
    return t(h[..., :PLANE]), t(h[..., PLANE:])


def _postmix_kernel(*refs, dils, chunk, final):
    it = iter(refs)
    x1_ref = next(it)
    att_refs = [(next(it), next(it)) for _ in dils] if len(dils) > 1 else [(next(it), None)]
    y_ref, uf_ref, p_ref = next(it), next(it), next(it)
    (natt_ref, nssm_ref, dskip_ref, wglu_ref, bglu_ref, wout_ref, nf_ref, wg_ref, wu_ref, wd_ref,
     nple_ref, wpg_ref, wpp_ref) = (next(it) for _ in range(13))
    perm_t = {d: next(it) for d in sorted((set(dils) | {chunk}) - {1})}
    nfin_ref = next(it) if final else None
    out_ref, ybuf = next(it), next(it)
    tm = x1_ref.shape[0]

    def natural(ref, d):
        val = ref[...] if d == 1 else ref[0].reshape(tm, D_ATT)
        return val if d == 1 else _permute_f32(perm_t[d], val)

    if len(dils) == 1:
        att = att_refs[0][0][...]
    else:
        outs = [natural(o, d) for d, (o, _) in zip(dils, att_refs)]
        lses = [natural(l, d) for d, (_, l) in zip(dils, att_refs)]
        top = functools.reduce(jnp.maximum, lses)
        ws = [jnp.exp(l - top) for l in lses]
        att = sum(w * o for w, o in zip(ws, outs)) / sum(ws)
    a_n = _rms(att, natt_ref[...])
    rows = tm // chunk
    for t in range(chunk):
        for j in range(N_JT):
            ybuf[t * rows:(t + 1) * rows, j * LANE:(j + 1) * LANE] = y_ref[j, :, t * LANE:(t + 1) * LANE]
    y = _permute_f32(perm_t[chunk], ybuf[...]) + dskip_ref[...] * uf_ref[...]
    z = jax.nn.gelu(y)
    s = z * jax.nn.sigmoid(_dot(z.astype(BF16), wglu_ref[...]) + bglu_ref[...])
    s_n = _rms(s, nssm_ref[...])
    mixed = jnp.concatenate([a_n, s_n], axis=-1).astype(BF16)
    x2 = x1_ref[...] + _dot(mixed, wout_ref[...])
    x3 = _ffn(x2, nf_ref[...], wg_ref, wu_ref, wd_ref)
    gate = jax.nn.sigmoid(_dot(_rms(x3, nple_ref[...]).astype(BF16), wpg_ref[...]))
    x4 = x3 + gate * _dot(p_ref[...].astype(BF16), wpp_ref[...])
    out_ref[...] = _rms(x4, nfin_ref[...]) if final else x4


def _postmix(x1, att_planes, dils, y, uf, p, layer, weights, perms_t, nfin, tm, seq, chunk):
    m = x1.shape[0]
    per_seq = seq // tm
    final = nfin is not None
    row = lambda w: pl.BlockSpec((tm, w), lambda i: (i, 0))
    att_args, att_specs = [], []
    for d, (o, l) in zip(dils, att_planes):
        spec = row(D_ATT) if d == 1 else pl.BlockSpec((1, d, tm // d, D_ATT),
                                                       lambda i: (i // per_seq, 0, i % per_seq, 0))
        for a in ((o, l) if len(dils) > 1 else (o,)):
            att_args.append(a.reshape(m, D_ATT) if d == 1 else a)
            att_specs.append(spec)
    consts, const_specs = _split_consts(list(weights) + [perms_t[d] for d in sorted((set(dils) | {chunk}) - {1})]
                                        + ([nfin] if final else []))
    args = [x1] + att_args + [y, uf, p] + consts
    in_specs = ([row(D_MODEL)] + att_specs
                + [pl.BlockSpec((N_JT, tm // chunk, chunk * LANE), lambda i: (0, i, 0)), row(D_SSM),
                   pl.BlockSpec((pl.Squeezed(), tm, D_PLE), lambda i: (layer, i, 0))]
                + const_specs)
    return pl.pallas_call(
        functools.partial(_postmix_kernel, dils=tuple(dils), chunk=chunk, final=final),
        grid=(m // tm,),
        in_specs=in_specs,
        out_specs=row(D_MODEL),
        out_shape=jax.ShapeDtypeStruct((m, D_MODEL), F32),
        scratch_shapes=[pltpu.VMEM((tm, D_SSM), F32)],
        compiler_params=_params(1),
        name="postmix",
    )(*args)


def _t5_bucket(dist):
    dist = np.asarray(dist, dtype=np.int64)
    exact = N_BUCKETS // 2
    ratio = np.log(np.maximum(dist, 1) / exact) / np.log(MAX_DISTANCE / exact)
    large = np.minimum(exact + (ratio * (N_BUCKETS - exact)).astype(np.int64), N_BUCKETS - 1)
    return np.where(dist < exact, dist, large).astype(np.int32)


def _branch_bias(rel_bias, dil):
    return rel_bias[_t5_bucket(np.arange(N_STEP + 1) * dil)].T.astype(F32)


def _prompt_bias_row(rel_bias, dil):
    step = N_STEP - np.arange(2 * N_STEP)
    row = rel_bias[_t5_bucket(np.maximum(step, 0) * dil)].T.astype(F32)
    return jnp.where((step >= 0)[None], row, NEG)


def _sample_table(biases, step_of):
    n_br, t_new, cols = step_of.shape
    tabs = []
    for bi in range(n_br):
        vals = biases[bi][:, np.maximum(step_of[bi], 0)]
        tab = jnp.where((step_of[bi] >= 0)[None], vals, NEG)
        tabs.append(jnp.concatenate([tab, jnp.zeros((N_HEADS, Q_PAD - t_new, cols), F32)], axis=1))
    return jnp.stack(tabs)


def kernel(x_prompt, x_sample, p_prompt, p_sample, cache_k, cache_v, state_ssm_re, state_ssm_im,
           rel_bias, w_in, w_out, norm_mix, norm_att_out, norm_ssm_out, norm_ffn, ffn_w_gate,
           ffn_w_up, ffn_w_down, ssm_a_re, ssm_a_im, ssm_log_dt, ssm_b_re, ssm_b_im, ssm_c_re,
           ssm_c_im, ssm_d, w_glu, b_glu, norm_ple, w_ple_gate, w_ple_proj, norm_final):
    depth = w_in.shape[0]
    batch, seq, _ = x_prompt.shape
    dec_batch, dec_seq, _ = x_sample.shape
    cache_len = cache_k.shape[2]
    keep = min(BRANCHES[-1][0], seq)
    dils = tuple(d for _, d in BRANCHES)
    chunk_p = dils[-1]
    tm_p = 256
    tm_s = dec_batch * dec_seq
    assert seq % tm_p == 0 and chunk_p in dils and dec_seq % TOK_PER_PAIR == 0 and tm_s % 8 == 0

    xp = x_prompt.reshape(batch * seq, D_MODEL)
    xs = x_sample.reshape(tm_s, D_MODEL)
    biases = [_branch_bias(rel_bias, d) for d in dils]
    tabs_p = [_prompt_bias_row(rel_bias, d) for d in dils]
    tab_s = _sample_table(biases, _sample_steps(cache_len, dec_seq, tm_s))
    to_lanes = lambda a: a.transpose(0, 1, 3, 4, 2)
    from_lanes = lambda a: a.transpose(0, 1, 4, 2, 3)
    cache_kt, cache_vt = to_lanes(cache_k), to_lanes(cache_v)
    perms_p = {d: jnp.asarray(_perm_matrix(tm_p, d), BF16) for d in dils if d > 1}
    perms_pt = {d: jnp.asarray(_perm_matrix(tm_p, d).T, BF16) for d in dils if d > 1}
    perms_s = {dec_seq: jnp.asarray(_perm_matrix(tm_s, dec_seq), BF16)}
    perms_st = {dec_seq: jnp.asarray(_perm_matrix(tm_s, dec_seq).T, BF16)}
    row = lambda v: v.reshape(1, -1)
    bf = lambda a: a.astype(BF16)

    wg, wu, wd = bf(ffn_w_gate), bf(ffn_w_up), bf(ffn_w_down)
    win, wout, wglu, wpg, wpp = bf(w_in), bf(w_out), bf(w_glu), bf(w_ple_gate), bf(w_ple_proj)

    def layer_weights(i):
        pre = (row(norm_ffn[i, 0]), (wg, (i, 0)), (wu, (i, 0)), (wd, (i, 0)), row(norm_mix[i]), (win, (i,)))
        post = (row(norm_att_out[i]), row(norm_ssm_out[i]), row(ssm_d[i]), (wglu, (i,)), row(b_glu[i]),
                (wout, (i,)), row(norm_ffn[i, 1]), (wg, (i, 1)), (wu, (i, 1)), (wd, (i, 1)),
                row(norm_ple[i]), (wpg, (i,)), (wpp, (i,)))
        nfin = row(norm_final) if i == depth - 1 else None
        ssm_par = (ssm_a_re[i], ssm_a_im[i], ssm_log_dt[i], ssm_b_re[i], ssm_b_im[i], ssm_c_re[i], ssm_c_im[i])
        return pre, post, nfin, ssm_par

    k_news, v_news, rss, iss = [], [], [], []
    for i in range(depth):
        pre, post, nfin, ssm_par = layer_weights(i)
        x1, q, _, _, kf, vf, uf, uc = _premix(xs, pre, perms_s, tm_s, tm_s, tm_s, (), dec_seq)
        att = _attn_sample(q, kf[0, 0], vf[0, 0], cache_kt, cache_vt, i, tab_s, dec_batch, dec_seq)
        h0 = _state_to_tiles(state_ssm_re[i], state_ssm_im[i], 1)
        y, h_last = _ssm(uc, h0, _ssm_tables(*ssm_par, dec_seq), dec_seq, dec_batch, False)
        xs = _postmix(x1, [(att, None)], (1,), y, uf, p_sample.reshape(depth, tm_s, D_PLE), i, post, perms_st,
                      nfin, tm_s, tm_s, dec_seq)
        k_news.append(kf[0, 0])
        v_news.append(vf[0, 0])
        re, im = _tiles_to_state(h_last)
        rss.append(re)
        iss.append(im)

    assert depth >= 2
    windows = [(cache_kt, jnp.stack(k_news), dec_seq), (cache_vt, jnp.stack(v_news), dec_seq)]
    kv_state, rp, ip = None, [], []
    for i in range(depth):
        pre, post, nfin, ssm_par = layer_weights(i)
        outs = list(_premix(xp, pre, perms_p, tm_p, seq, keep, dils[1:], chunk_p,
                            window=windows[i] if i < 2 else None, prev_state=kv_state))
        if i < 2:
            windows[i] = outs.pop()
        x1, q, k, v = outs[:4]
        kf, vf, uf, uc = outs[-4:]
        kv_state = (kf, vf)
        nat = lambda a: a.reshape(batch, 1, seq, D_ATT)
        qkv = [(nat(q), nat(k), nat(v))] + [tuple(outs[4 + 3 * n:7 + 3 * n]) for n in range(len(dils) - 1)]
        planes = [_attn_prompt_branch(*qkv[bi], tabs_p[bi]) for bi in range(len(dils))]
        zeros = jnp.zeros((batch, N_GROUPS, SSM_STATE), F32)
        y, h_last = _ssm(uc, _state_to_tiles(zeros, zeros, batch), _ssm_tables(*ssm_par, chunk_p),
                         chunk_p, seq // chunk_p, True)
        xp = _postmix(x1, planes, dils, y, uf, p_prompt.reshape(depth, batch * seq, D_PLE), i, post, perms_pt,
                      nfin, tm_p, seq, chunk_p)
        re, im = _tiles_to_state(h_last)
        rp.append(re)
        ip.append(im)

    return (xp.reshape(batch, seq, D_MODEL), xs.reshape(dec_batch, dec_seq, D_MODEL),
            from_lanes(kv_state[0]), from_lanes(kv_state[1]), jnp.stack(rp), jnp.stack(ip),
            from_lanes(windows[0]), from_lanes(windows[1]), jnp.stack(rss), jnp.stack(iss))
```
